```python
import math
import jax, jax.numpy as jnp
from jax import lax
import numpy as np

D_MODEL = 2048
BATCH = 8
SEQ = 4096
DEPTH = 1
DEC_BATCH = 8
DEC_SEQ = 2048
PAST_LEN = 128

D_RNN = D_MODEL // 2
RNN_BLOCKS = 8
BLOCK_W = D_RNN // RNN_BLOCKS
CONV_W = 4
CONV_LEFT = 2
C_RG = 8.0
N_HEADS = D_MODEL // 128
QK_NOPE = 128
QK_ROPE = 64
QK_HEAD = QK_NOPE + QK_ROPE
V_HEAD = 128
Q_LORA = D_MODEL // 4
KV_LORA = D_MODEL // 4
ROPE_THETA = 10000.0
Q_BLOCK = 128
N_BRANCH = 2
D_IN = 2 * D_RNN + Q_LORA + KV_LORA + QK_ROPE + N_BRANCH * D_MODEL
N_EXPERTS = 32
TOP_K = 4
D_FF = D_MODEL
SWIGLU_LIMIT = 7.0
SWIGLU_ALPHA = 1.702
ROW_BLOCK = 128
EPS = 1e-6

kernel_name = "hybrid_rglru_mla_moe_encoder"


def rmsnorm(x, g):
    xf = x.astype(jnp.float32)
    y = xf * lax.rsqrt(jnp.mean(xf * xf, axis=-1, keepdims=True) + EPS) * g.astype(jnp.float32)
    return y.astype(x.dtype)


def rope_tables(seq):
    half = QK_ROPE // 2
    freqs = ROPE_THETA ** (-jnp.arange(half, dtype=jnp.float32) / half)
    ang = jnp.arange(seq, dtype=jnp.float32)[:, None] * freqs[None, :]
    return jnp.cos(ang), jnp.sin(ang)


def apply_rope(x, cos, sin):
    half = QK_ROPE // 2
    xf = x.astype(jnp.float32)
    x1, x2 = xf[..., :half], xf[..., half:]
    return jnp.concatenate([x1 * cos - x2 * sin, x2 * cos + x1 * sin], axis=-1).astype(x.dtype)


def _lin_combine(c1, c2):
    a1, b1 = c1
    a2, b2 = c2
    return a1 * a2, a2 * b1 + b2


def bidir_rglru(xc, w_a, b_a, w_x, b_x, lam):
    B, S, _ = xc.shape
    xf = xc.astype(jnp.float32)
    xb = xf.reshape(B, S, RNN_BLOCKS, BLOCK_W)
    r = jax.nn.sigmoid(jnp.einsum('bsnd,cnde->cbsne', xb, w_a.astype(jnp.float32)).reshape(2, B, S, D_RNN)
                       + b_a.astype(jnp.float32)[:, None, None, :])
    i = jax.nn.sigmoid(jnp.einsum('bsnd,cnde->cbsne', xb, w_x.astype(jnp.float32)).reshape(2, B, S, D_RNN)
                       + b_x.astype(jnp.float32)[:, None, None, :])
    log_a = -C_RG * r * jax.nn.softplus(-lam.astype(jnp.float32))[:, None, None, :]
    a = jnp.exp(log_a)
    b = jnp.sqrt(-jnp.expm1(2.0 * log_a)) * (i * xf[None])
    _, h_fwd = lax.associative_scan(_lin_combine, (a[0], b[0]), axis=1)
    _, h_bwd = lax.associative_scan(_lin_combine, (a[1], b[1]), axis=1, reverse=True)
    return (h_fwd + h_bwd).astype(xc.dtype)


def mla_attention(q_nope, q_rope, k_nope, k_rope, v):
    B, S, H, _ = q_nope.shape
    nq = S // Q_BLOCK
    scale = 1.0 / math.sqrt(QK_HEAD)
    qn = q_nope.reshape(B, nq, Q_BLOCK, H, QK_NOPE).transpose(1, 0, 2, 3, 4)
    qr = q_rope.reshape(B, nq, Q_BLOCK, H, QK_ROPE).transpose(1, 0, 2, 3, 4)

    def block(args):
        qn_b, qr_b = args
        s = (jnp.einsum('bqhd,bkhd->bhqk', qn_b, k_nope).astype(jnp.float32)
             + jnp.einsum('bqhr,bkr->bhqk', qr_b, k_rope).astype(jnp.float32))
        p = jax.nn.softmax(s * scale, axis=-1).astype(v.dtype)
        return jnp.einsum('bhqk,bkhd->bqhd', p, v)

    o = lax.map(block, (qn, qr))
    return o.transpose(1, 0, 2, 3, 4).reshape(B, S, H * V_HEAD)


def moe_ffn(x2d, w_router, b_router, w_gate_up, b_gate_up, w_down, b_down):
    n, d = x2d.shape
    logits = x2d.astype(jnp.float32) @ w_router.astype(jnp.float32) + b_router.astype(jnp.float32)
    top_val, top_idx = lax.top_k(logits, TOP_K)
    gate = jax.nn.softmax(top_val, axis=-1)
    n_assign = n * TOP_K
    flat_e = top_idx.reshape(-1).astype(jnp.int32)
    flat_tok = jnp.arange(n_assign, dtype=jnp.int32) // TOP_K
    order = jnp.argsort(flat_e)
    sorted_e = flat_e[order]
    counts = jnp.bincount(flat_e, length=N_EXPERTS).astype(jnp.int32)
    start = jnp.cumsum(counts) - counts
    padded = (counts + ROW_BLOCK - 1) // ROW_BLOCK * ROW_BLOCK
    pad_end = jnp.cumsum(padded)
    pad_start = pad_end - padded
    dest = pad_start[sorted_e] + jnp.arange(n_assign, dtype=jnp.int32) - start[sorted_e]
    n_blocks = -(-n_assign // ROW_BLOCK) + N_EXPERTS
    rows = n_blocks * ROW_BLOCK
    row_tok = jnp.full((rows,), n, jnp.int32).at[dest].set(flat_tok[order])
    row_w = jnp.zeros((rows,), jnp.float32).at[dest].set(gate.reshape(-1)[order])
    block_e = jnp.minimum(jnp.searchsorted(pad_end, jnp.arange(n_blocks, dtype=jnp.int32) * ROW_BLOCK,
                                           side='right'), N_EXPERTS - 1)
    x_pad = jnp.concatenate([x2d, jnp.zeros((1, d), x2d.dtype)], axis=0)

    def expert_block(args):
        tok, w, e = args
        xb = x_pad[tok]
        gu = xb @ w_gate_up[e] + b_gate_up[e]
        g = jnp.minimum(gu[:, 0::2], SWIGLU_LIMIT)
        u = jnp.clip(gu[:, 1::2], -SWIGLU_LIMIT, SWIGLU_LIMIT)
        hdn = g * jax.nn.sigmoid(SWIGLU_ALPHA * g) * (u + 1.0)
        return ((hdn @ w_down[e] + b_down[e]) * w[:, None]).astype(x2d.dtype)

    out = lax.map(expert_block, (row_tok.reshape(n_blocks, ROW_BLOCK),
                                 row_w.reshape(n_blocks, ROW_BLOCK), block_e))
    y = jnp.zeros((n + 1, d), x2d.dtype).at[row_tok].add(out.reshape(rows, d))
    return y[:n]


def trunk(x, norm_mix_g, w_in, conv_w, conv_b, rg_w_a, rg_b_a, rg_w_x, rg_b_x, rg_lambda,
          w_proj_a, q_norm_g, w_uq, kv_norm_g, w_uk, w_uv, w_proj_b, w_out,
          norm_ffn_g, w_router, b_router, w_gate_up, b_gate_up, w_down, b_down, norm_final_g):
    B, S, _ = x.shape
    cos, sin = rope_tables(S)
    offs = np.cumsum([2 * 0 + D_RNN, D_RNN, Q_LORA, KV_LORA, QK_ROPE, D_MODEL]).tolist()
    for l in range(DEPTH):
        u = rmsnorm(x, norm_mix_g[l])
        z = u @ w_in[l]
        xa, ga, cq, ckv, kr, gl_a, gl_b = jnp.split(z, offs, axis=-1)
        xp = jnp.pad(xa, ((0, 0), (CONV_LEFT, CONV_W - 1 - CONV_LEFT), (0, 0)))
        xc = sum(xp[:, k:k + S] * conv_w[l][k] for k in range(CONV_W)) + conv_b[l]
        h = bidir_rglru(xc, rg_w_a[l], rg_b_a[l], rg_w_x[l], rg_b_x[l], rg_lambda[l])
        ya = (h * jax.nn.gelu(ga)) @ w_proj_a[l]
        cq = rmsnorm(cq, q_norm_g[l])
        q = (cq @ w_uq[l]).reshape(B, S, N_HEADS, QK_HEAD)
        q_nope = q[..., :QK_NOPE]
        q_rope = apply_rope(q[..., QK_NOPE:], cos[:, None, :], sin[:, None, :])
        ckv = rmsnorm(ckv, kv_norm_g[l])
        k_nope = (ckv @ w_uk[l]).reshape(B, S, N_HEADS, QK_NOPE)
        v = (ckv @ w_uv[l]).reshape(B, S, N_HEADS, V_HEAD)
        k_rope = apply_rope(kr, cos, sin)
        yb = mla_attention(q_nope, q_rope, k_nope, k_rope, v) @ w_proj_b[l]
        m = jax.nn.sigmoid(gl_a) * ya + jax.nn.sigmoid(gl_b) * yb
        x = x + m @ w_out[l]
        un = rmsnorm(x, norm_ffn_g[l]).reshape(B * S, D_MODEL)
        x = x + moe_ffn(un, w_router[l], b_router[l], w_gate_up[l], b_gate_up[l],
                        w_down[l], b_down[l]).reshape(B, S, D_MODEL)
    return rmsnorm(x, norm_final_g)


def setup_inputs(seed: int = 0) -> dict:
    key = jax.random.key(seed)
    ks = jax.random.split(key, 32)
    f32 = jnp.float32

    def nrm(k, shape, scale):
        return jax.random.normal(k, shape, f32) * scale

    L = DEPTH
    a0 = jax.random.uniform(ks[9], (L, 2, D_RNN), f32, 0.9, 0.999)
    s0 = a0 ** (1.0 / C_RG)
    lam = jnp.log(s0) - jnp.log1p(-s0)
    return {
        "x_prompt": nrm(ks[0], (BATCH, SEQ, D_MODEL), 1.0),
        "x_sample": nrm(ks[1], (DEC_BATCH, DEC_SEQ, D_MODEL), 1.0),
        "norm_mix_g": 1.0 + nrm(ks[2], (L, D_MODEL), 0.01),
        "w_in": nrm(ks[3], (L, D_MODEL, D_IN), D_MODEL ** -0.5),
        "conv_w": nrm(ks[4], (L, CONV_W, D_RNN), CONV_W ** -0.5),
        "conv_b": nrm(ks[5], (L, D_RNN), 0.01),
        "rg_w_a": nrm(ks[6], (L, 2, RNN_BLOCKS, BLOCK_W, BLOCK_W), BLOCK_W ** -0.5),
        "rg_b_a": nrm(ks[7], (L, 2, D_RNN), 0.01),
        "rg_w_x": nrm(ks[8], (L, 2, RNN_BLOCKS, BLOCK_W, BLOCK_W), BLOCK_W ** -0.5),
        "rg_b_x": nrm(ks[10], (L, 2, D_RNN), 0.01),
        "rg_lambda": lam,
        "w_proj_a": nrm(ks[11], (L, D_RNN, D_MODEL), D_RNN ** -0.5),
        "q_norm_g": 1.0 + nrm(ks[12], (L, Q_LORA), 0.01),
        "w_uq": nrm(ks[13], (L, Q_LORA, N_HEADS * QK_HEAD), Q_LORA ** -0.5),
        "kv_norm_g": 1.0 + nrm(ks[14], (L, KV_LORA), 0.01),
        "w_uk": nrm(ks[15], (L, KV_LORA, N_HEADS * QK_NOPE), KV_LORA ** -0.5),
        "w_uv": nrm(ks[16], (L, KV_LORA, N_HEADS * V_HEAD), KV_LORA ** -0.5),
        "w_proj_b": nrm(ks[17], (L, N_HEADS * V_HEAD, D_MODEL), (N_HEADS * V_HEAD) ** -0.5),
        "w_out": nrm(ks[18], (L, D_MODEL, D_MODEL), D_MODEL ** -0.5),
        "norm_ffn_g": 1.0 + nrm(ks[19], (L, D_MODEL), 0.01),
        "w_router": nrm(ks[20], (L, D_MODEL, N_EXPERTS), D_MODEL ** -0.5),
        "b_router": nrm(ks[21], (L, N_EXPERTS), 0.01),
        "w_gate_up": nrm(ks[22], (L, N_EXPERTS, D_MODEL, 2 * D_FF), D_MODEL ** -0.5),
        "b_gate_up": nrm(ks[23], (L, N_EXPERTS, 2 * D_FF), 0.01),
        "w_down": nrm(ks[24], (L, N_EXPERTS, D_FF, D_MODEL), D_FF ** -0.5),
        "b_down": nrm(ks[25], (L, N_EXPERTS, D_MODEL), 0.01),
        "norm_final_g": 1.0 + nrm(ks[26], (D_MODEL,), 0.01),
    }


def reference(x_prompt, x_sample, norm_mix_g, w_in, conv_w, conv_b, rg_w_a, rg_b_a, rg_w_x, rg_b_x,
              rg_lambda, w_proj_a, q_norm_g, w_uq, kv_norm_g, w_uk, w_uv, w_proj_b, w_out,
              norm_ffn_g, w_router, b_router, w_gate_up, b_gate_up, w_down, b_down, norm_final_g):
    y_prompt = trunk(x_prompt, norm_mix_g, w_in, conv_w, conv_b, rg_w_a, rg_b_a, rg_w_x, rg_b_x, rg_lambda,
                     w_proj_a, q_norm_g, w_uq, kv_norm_g, w_uk, w_uv, w_proj_b, w_out,
                     norm_ffn_g, w_router, b_router, w_gate_up, b_gate_up, w_down, b_down, norm_final_g)
    y_sample = trunk(x_sample, norm_mix_g, w_in, conv_w, conv_b, rg_w_a, rg_b_a, rg_w_x, rg_b_x, rg_lambda,
                     w_proj_a, q_norm_g, w_uq, kv_norm_g, w_uk, w_uv, w_proj_b, w_out,
                     norm_ffn_g, w_router, b_router, w_gate_up, b_gate_up, w_down, b_down, norm_final_g)
    return (y_prompt, y_sample)
```

```python
import functools
import math

import jax
import jax.numpy as jnp
from jax import lax
from jax.experimental import pallas as pl
from jax.experimental.pallas import tpu as pltpu

F32 = jnp.float32
BF16 = jnp.bfloat16
I32 = jnp.int32

D_MODEL = 2048
D_RNN = 1024
RNN_BLOCKS = 8
BLOCK_W = 128
C_RG = 8.0
N_HEADS = 16
QK_NOPE = 128
QK_ROPE = 64
QK_HEAD = QK_NOPE + QK_ROPE
V_HEAD = 128
Q_LORA = 512
KV_LORA = 512
ROPE_THETA = 10000.0
N_EXPERTS = 32
TOP_K = 4
D_FF = 2048
SWIGLU_LIMIT = 7.0
SWIGLU_ALPHA = 1.702
EPS = 1e-6

LANES = 128
VMEM_LIMIT = 56 * 1024 * 1024
NEG_BIG = -1e30

Z_GLA, Z_GLB, Z_XA, Z_GA, Z_CQ, Z_CKV = 0, 2048, 4096, 5120, 6144, 6656
Z_COLS = 7168


def _cparams(sem):
    return pltpu.CompilerParams(dimension_semantics=sem, vmem_limit_bytes=VMEM_LIMIT)


def _tile(n, pref):
    t = min(n, pref)
    assert n % t == 0, (n, pref)
    return t


def _rms(x, g):
    return x * lax.rsqrt(jnp.mean(x * x, axis=-1, keepdims=True) + EPS) * g


def _in_proj_body(x_ref, g_ref, w_ref, wk_ref, z_ref, zk_ref, u_ref):
    @pl.when(pl.program_id(1) == 0)
    def _():
        u = _rms(x_ref[...], g_ref[...]).astype(BF16)
        u_ref[...] = u
        zk_ref[...] = jnp.dot(u, wk_ref[...], preferred_element_type=F32).astype(BF16)

    z_ref[...] = jnp.dot(u_ref[...], w_ref[...], preferred_element_type=F32).astype(BF16)


def _in_proj(x2d, g, w_main, w_k2):
    n = x2d.shape[0]
    tm = _tile(n, 1024)
    tn = 1024
    return pl.pallas_call(
        _in_proj_body,
        grid=(n // tm, Z_COLS // tn),
        in_specs=[
            pl.BlockSpec((tm, D_MODEL), lambda i, j: (i, 0)),
            pl.BlockSpec((1, D_MODEL), lambda i, j: (0, 0)),
            pl.BlockSpec((D_MODEL, tn), lambda i, j: (0, j)),
            pl.BlockSpec((D_MODEL, LANES), lambda i, j: (0, 0)),
        ],
        out_specs=[
            pl.BlockSpec((tm, tn), lambda i, j: (i, j)),
            pl.BlockSpec((tm, LANES), lambda i, j: (i, 0)),
        ],
        out_shape=[
            jax.ShapeDtypeStruct((n, Z_COLS), BF16),
            jax.ShapeDtypeStruct((n, LANES), BF16),
        ],
        scratch_shapes=[pltpu.VMEM((tm, D_MODEL), BF16)],
        compiler_params=_cparams(("arbitrary", "arbitrary")),
        name="in_proj",
    )(x2d, g, w_main, w_k2)


HALO = 16
SCAN_ROWS = 128


def _sigmoid(x):
    return 1.0 / (1.0 + jnp.exp(-x))


def _gelu_tanh(x):
    return 0.5 * x * (1.0 + jnp.tanh(math.sqrt(2.0 / math.pi) * (x + 0.044715 * (x * x * x))))


def _scan_chunk(a, b, reverse):
    rows = a.shape[0]
    row = lax.broadcasted_iota(I32, a.shape, 0)
    d = 1
    while d < rows:
        if d < 8:
            shift = (rows - d) if reverse else d
            a_sh = pltpu.roll(a, shift, axis=0)
            b_sh = pltpu.roll(b, shift, axis=0)
            keep = (row < rows - d) if reverse else (row >= d)
            a_sh = jnp.where(keep, a_sh, 1.0)
            b_sh = jnp.where(keep, b_sh, 0.0)
        else:
            one = jnp.ones((d, a.shape[1]), F32)
            zero = jnp.zeros((d, a.shape[1]), F32)
            if reverse:
                a_sh = jnp.concatenate([a[d:], one], axis=0)
                b_sh = jnp.concatenate([b[d:], zero], axis=0)
            else:
                a_sh = jnp.concatenate([one, a[:-d]], axis=0)
                b_sh = jnp.concatenate([zero, b[:-d]], axis=0)
        b = a * b_sh + b
        a = a * a_sh
        d *= 2
    return a, b


def _rglru_body(xa_ref, ga_ref, cw_ref, cb_ref, w4_ref, b4_ref, lam_ref, o_ref,
                a0_ref, b0_ref, a1_ref, b1_ref, *, seq, tc):
    nchunk = seq // tc
    lam = lam_ref[0]
    sp = jnp.maximum(-lam, 0.0) + jnp.log1p(jnp.exp(-jnp.abs(lam)))
    sp0 = sp[:, :BLOCK_W]
    sp1 = sp[:, BLOCK_W:]
    cw = cw_ref[...]
    cb = cb_ref[...]
    w4 = w4_ref[0]
    b4 = b4_ref[0]

    def gates(j, carry):
        r0 = pl.multiple_of(j * tc, tc)
        cur = xa_ref[pl.ds(r0, tc), :].astype(F32)
        prev_start = pl.multiple_of(jnp.maximum(r0 - HALO, 0), HALO)
        next_start = pl.multiple_of(jnp.minimum(r0 + tc, seq - HALO), HALO)
        prev = xa_ref[pl.ds(prev_start, HALO), :].astype(F32)
        nxt = xa_ref[pl.ds(next_start, HALO), :].astype(F32)
        prev = jnp.where(j > 0, prev, 0.0)
        nxt = jnp.where(j < nchunk - 1, nxt, 0.0)
        ext = jnp.concatenate([prev, cur, nxt], axis=0)
        n_ext = tc + 2 * HALO
        xm2 = pltpu.roll(ext, 2, axis=0)[HALO:HALO + tc]
        xm1 = pltpu.roll(ext, 1, axis=0)[HALO:HALO + tc]
        xp1 = pltpu.roll(ext, n_ext - 1, axis=0)[HALO:HALO + tc]
        xc = xm2 * cw[0:1] + xm1 * cw[1:2] + cur * cw[2:3] + xp1 * cw[3:4] + cb
        pre = jnp.dot(xc.astype(BF16), w4, preferred_element_type=F32) + b4
        for d, (a_ref, b_ref, spd) in enumerate(((a0_ref, b0_ref, sp0), (a1_ref, b1_ref, sp1))):
            r = _sigmoid(pre[:, (2 * d) * BLOCK_W:(2 * d + 1) * BLOCK_W])
            i = _sigmoid(pre[:, (2 * d + 1) * BLOCK_W:(2 * d + 2) * BLOCK_W])
            a = jnp.exp((-C_RG) * r * spd)
            b = jnp.sqrt(1.0 - a * a) * (i * xc)
            a_ref[pl.ds(r0, tc), :] = a
            b_ref[pl.ds(r0, tc), :] = b
        return carry

    lax.fori_loop(0, nchunk, gates, 0)

    nscan = seq // SCAN_ROWS

    def bwd(jj, h):
        j = nscan - 1 - jj
        r0 = pl.multiple_of(j * SCAN_ROWS, SCAN_ROWS)
        A, B = _scan_chunk(a1_ref[pl.ds(r0, SCAN_ROWS), :], b1_ref[pl.ds(r0, SCAN_ROWS), :], True)
        hh = A * h + B
        b1_ref[pl.ds(r0, SCAN_ROWS), :] = hh
        return hh[0:1, :]

    lax.fori_loop(0, nscan, bwd, jnp.zeros((1, BLOCK_W), F32))

    def fwd(j, h):
        r0 = pl.multiple_of(j * SCAN_ROWS, SCAN_ROWS)
        A, B = _scan_chunk(a0_ref[pl.ds(r0, SCAN_ROWS), :], b0_ref[pl.ds(r0, SCAN_ROWS), :], False)
        hh = A * h + B
        ga = ga_ref[pl.ds(r0, SCAN_ROWS), :].astype(F32)
        o_ref[pl.ds(r0, SCAN_ROWS), :] = ((hh + b1_ref[pl.ds(r0, SCAN_ROWS), :]) * _gelu_tanh(ga)).astype(BF16)
        return hh[SCAN_ROWS - 1:SCAN_ROWS, :]

    lax.fori_loop(0, nscan, fwd, jnp.zeros((1, BLOCK_W), F32))


def _rglru(z, conv_w, conv_b, w4, b4, lam4, batch, seq):
    n = batch * seq
    tc = _tile(seq, 256)
    xa_blk = Z_XA // BLOCK_W
    ga_blk = Z_GA // BLOCK_W
    return pl.pallas_call(
        functools.partial(_rglru_body, seq=seq, tc=tc),
        grid=(batch, RNN_BLOCKS),
        in_specs=[
            pl.BlockSpec((seq, BLOCK_W), lambda b, c: (b, xa_blk + c)),
            pl.BlockSpec((seq, BLOCK_W), lambda b, c: (b, ga_blk + c)),
            pl.BlockSpec((4, BLOCK_W), lambda b, c: (0, c)),
            pl.BlockSpec((1, BLOCK_W), lambda b, c: (0, c)),
            pl.BlockSpec((1, BLOCK_W, 4 * BLOCK_W), lambda b, c: (c, 0, 0)),
            pl.BlockSpec((1, 1, 4 * BLOCK_W), lambda b, c: (c, 0, 0)),
            pl.BlockSpec((1, 1, 2 * BLOCK_W), lambda b, c: (c, 0, 0)),
        ],
        out_specs=pl.BlockSpec((seq, BLOCK_W), lambda b, c: (b, c)),
        out_shape=jax.ShapeDtypeStruct((n, D_RNN), BF16),
        scratch_shapes=[pltpu.VMEM((seq, BLOCK_W), F32) for _ in range(4)],
        compiler_params=_cparams(("arbitrary", "arbitrary")),
        name="rglru",
    )(z, z, conv_w, conv_b, w4, b4, lam4)


KV_CHUNK = 512


def _qkv_body(cq_ref, ckv_ref, zk_ref, ct_ref, st_ref, qg_ref, kvg_ref,
              wqn_ref, wqr_ref, wk_ref, wvt_ref,
              qn_ref, qr_ref, kn_ref, kr_ref, vt_ref):
    scale = 1.0 / math.sqrt(QK_HEAD)
    ct = ct_ref[...]
    st = st_ref[...]
    cqn = _rms(cq_ref[...].astype(F32), qg_ref[...]).astype(BF16)
    ckvn = _rms(ckv_ref[...].astype(F32), kvg_ref[...]).astype(BF16)
    qn = jnp.dot(cqn, wqn_ref[...], preferred_element_type=F32)
    qn_ref[...] = (qn * scale).astype(BF16)
    qp = jnp.dot(cqn, wqr_ref[...], preferred_element_type=F32)
    cts = ct * scale
    sts = st * scale
    for h in range(N_HEADS):
        p = qp[:, h * LANES:(h + 1) * LANES]
        qr_ref[:, h * LANES:(h + 1) * LANES] = (p * cts + pltpu.roll(p, LANES // 2, axis=1) * sts).astype(BF16)
    kn_ref[...] = jnp.dot(ckvn, wk_ref[...], preferred_element_type=F32).astype(BF16)
    zk = zk_ref[...].astype(F32)
    kr_ref[...] = (zk * ct + pltpu.roll(zk, LANES // 2, axis=1) * st).astype(BF16)
    vt = lax.dot_general(wvt_ref[...], ckvn, (((1,), (1,)), ((), ())), preferred_element_type=F32)
    vt_ref[0] = vt.astype(BF16)


def _qkv(z, zk, ctab, stab, qg, kvg, wqn, wqr, wk, wvt, seq):
    n = z.shape[0]
    t = _tile(seq, KV_CHUNK)
    spt = seq // t
    cq_blk = Z_CQ // Q_LORA
    ckv_blk = Z_CKV // KV_LORA
    hd = N_HEADS * LANES
    const = lambda i: (0, 0)
    return pl.pallas_call(
        _qkv_body,
        grid=(n // t,),
        in_specs=[
            pl.BlockSpec((t, Q_LORA), lambda i: (i, cq_blk)),
            pl.BlockSpec((t, KV_LORA), lambda i: (i, ckv_blk)),
            pl.BlockSpec((t, LANES), lambda i: (i, 0)),
            pl.BlockSpec((t, LANES), lambda i: (i % spt, 0)),
            pl.BlockSpec((t, LANES), lambda i: (i % spt, 0)),
            pl.BlockSpec((1, Q_LORA), const),
            pl.BlockSpec((1, KV_LORA), const),
            pl.BlockSpec((Q_LORA, hd), const),
            pl.BlockSpec((Q_LORA, hd), const),
            pl.BlockSpec((KV_LORA, hd), const),
            pl.BlockSpec((hd, KV_LORA), const),
        ],
        out_specs=[
            pl.BlockSpec((t, hd), lambda i: (i, 0)),
            pl.BlockSpec((t, hd), lambda i: (i, 0)),
            pl.BlockSpec((t, hd), lambda i: (i, 0)),
            pl.BlockSpec((t, LANES), lambda i: (i, 0)),
            pl.BlockSpec((1, hd, t), lambda i: (i, 0, 0)),
        ],
        out_shape=[
            jax.ShapeDtypeStruct((n, hd), BF16),
            jax.ShapeDtypeStruct((n, hd), BF16),
            jax.ShapeDtypeStruct((n, hd), BF16),
            jax.ShapeDtypeStruct((n, LANES), BF16),
            jax.ShapeDtypeStruct((n // t, hd, t), BF16),
        ],
        compiler_params=_cparams(("arbitrary",)),
        name="qkv",
    )(z, z, zk, ctab, stab, qg, kvg, wqn, wqr, wk, wvt)


def _attn_body(qn_ref, qr_ref, kn_ref, kr_ref, vt_ref, o_ref, m_ref, l_ref, acc_ref, *, nk, kc):
    q = jnp.concatenate([qn_ref[...], qr_ref[...]], axis=1)
    m_ref[...] = jnp.full(m_ref.shape, NEG_BIG, F32)
    l_ref[...] = jnp.zeros(l_ref.shape, F32)
    acc_ref[...] = jnp.zeros(acc_ref.shape, F32)

    def step(c, carry):
        r0 = pl.multiple_of(c * kc, kc)
        k = jnp.concatenate([kn_ref[pl.ds(r0, kc), :], kr_ref[pl.ds(r0, kc), :]], axis=1)
        s = lax.dot_general(k, q, (((1,), (1,)), ((), ())), preferred_element_type=F32)
        m_prev = m_ref[...]
        m_new = jnp.maximum(m_prev, jnp.max(s, axis=0, keepdims=True))
        alpha = jnp.exp(m_prev - m_new)
        p = jnp.exp(s - m_new)
        l_ref[...] = alpha * l_ref[...] + jnp.sum(p, axis=0, keepdims=True)
        acc_ref[...] = alpha * acc_ref[...] + jnp.dot(vt_ref[c], p.astype(BF16), preferred_element_type=F32)
        m_ref[...] = m_new
        return carry

    lax.fori_loop(0, nk, step, 0)
    o = acc_ref[...] * (1.0 / l_ref[...])
    o_ref[...] = o.T.astype(BF16)


def _attention(qn, qr, kn, kr, vt, batch, seq):
    n = batch * seq
    tq = _tile(seq, 512)
    kc = _tile(seq, KV_CHUNK)
    nq = seq // tq
    nk = seq // kc
    return pl.pallas_call(
        functools.partial(_attn_body, nk=nk, kc=kc),
        grid=(batch, N_HEADS, nq),
        in_specs=[
            pl.BlockSpec((tq, LANES), lambda b, h, i: (b * nq + i, h)),
            pl.BlockSpec((tq, LANES), lambda b, h, i: (b * nq + i, h)),
            pl.BlockSpec((seq, LANES), lambda b, h, i: (b, h)),
            pl.BlockSpec((seq, LANES), lambda b, h, i: (b, 0)),
            pl.BlockSpec((nk, V_HEAD, kc), lambda b, h, i: (b, h, 0)),
        ],
        out_specs=pl.BlockSpec((tq, V_HEAD), lambda b, h, i: (b * nq + i, h)),
        out_shape=jax.ShapeDtypeStruct((n, N_HEADS * V_HEAD), BF16),
        scratch_shapes=[
            pltpu.VMEM((1, tq), F32),
            pltpu.VMEM((1, tq), F32),
            pltpu.VMEM((V_HEAD, tq), F32),
        ],
        compiler_params=_cparams(("arbitrary", "arbitrary", "arbitrary")),
        name="attn",
    )(qn, qr, kn, kr, vt)


def _merge_body(hg_ref, at_ref, gla_ref, glb_ref, wa_ref, wb_ref, m_ref):
    ya = jnp.dot(hg_ref[...], wa_ref[...], preferred_element_type=F32)
    yb = jnp.dot(at_ref[...], wb_ref[...], preferred_element_type=F32)
    m = _sigmoid(gla_ref[...].astype(F32)) * ya + _sigmoid(glb_ref[...].astype(F32)) * yb
    m_ref[...] = m.astype(BF16)


def _merge(hg, attn, z, wa, wb):
    n = hg.shape[0]
    t = _tile(n, 256)
    const = lambda i: (0, 0)
    return pl.pallas_call(
        _merge_body,
        grid=(n // t,),
        in_specs=[
            pl.BlockSpec((t, D_RNN), lambda i: (i, 0)),
            pl.BlockSpec((t, D_MODEL), lambda i: (i, 0)),
            pl.BlockSpec((t, D_MODEL), lambda i: (i, Z_GLA // D_MODEL)),
            pl.BlockSpec((t, D_MODEL), lambda i: (i, Z_GLB // D_MODEL)),
            pl.BlockSpec((D_RNN, D_MODEL), const),
            pl.BlockSpec((D_MODEL, D_MODEL), const),
        ],
        out_specs=pl.BlockSpec((t, D_MODEL), lambda i: (i, 0)),
        out_shape=jax.ShapeDtypeStruct((n, D_MODEL), BF16),
        compiler_params=_cparams(("arbitrary",)),
        name="merge",
    )(hg, attn, z, z, wa, wb)


def _outproj_body(x_ref, m_ref, wo_ref, g_ref, wrt_ref, br_ref,
                  x1_ref, idx_ref, gate_ref, rank_ref, cnt_ref, carry_ref):
    t = x_ref.shape[0]

    @pl.when(pl.program_id(0) == 0)
    def _():
        carry_ref[...] = jnp.zeros(carry_ref.shape, F32)

    x1 = x_ref[...] + jnp.dot(m_ref[...], wo_ref[...], preferred_element_type=F32)
    x1_ref[...] = x1
    un = _rms(x1, g_ref[...])
    logits = lax.dot_general(wrt_ref[...], un, (((1,), (1,)), ((), ())),
                             precision=lax.Precision.HIGHEST,
                             preferred_element_type=F32) + br_ref[...]
    e_iota = lax.broadcasted_iota(I32, logits.shape, 0)
    vals, idxs, sels = [], [], []
    cur = logits
    for _ in range(TOP_K):
        mx = jnp.max(cur, axis=0, keepdims=True)
        idx = jnp.min(jnp.where(cur == mx, e_iota, N_EXPERTS), axis=0, keepdims=True)
        sel = e_iota == idx
        vals.append(mx)
        idxs.append(idx)
        sels.append(sel)
        cur = jnp.where(sel, -jnp.inf, cur)
    ex = [jnp.exp(v - vals[0]) for v in vals]
    inv = 1.0 / (ex[0] + ex[1] + ex[2] + ex[3])
    onehot = jnp.where(sels[0] | sels[1] | sels[2] | sels[3], 1.0, 0.0)
    row = lax.broadcasted_iota(I32, (t, t), 0)
    col = lax.broadcasted_iota(I32, (t, t), 1)
    upper = jnp.where(row < col, 1.0, 0.0).astype(BF16)
    prefix = jnp.dot(onehot.astype(BF16), upper, preferred_element_type=F32) + carry_ref[:, 0:1]
    for k in range(TOP_K):
        idx_ref[k:k + 1, :] = idxs[k]
        gate_ref[k:k + 1, :] = ex[k] * inv
        rank_ref[k:k + 1, :] = jnp.sum(jnp.where(sels[k], prefix, 0.0), axis=0, keepdims=True).astype(I32)
    gate_ref[TOP_K:, :] = jnp.zeros((gate_ref.shape[0] - TOP_K, t), F32)
    carry_ref[...] = carry_ref[...] + jnp.sum(onehot, axis=1, keepdims=True)
    cnt_ref[...] = carry_ref[...]


def _outproj(x2d, m, wo, g, wrt, br):
    n = x2d.shape[0]
    t = _tile(n, 256)
    const = lambda i: (0, 0)
    return pl.pallas_call(
        _outproj_body,
        grid=(n // t,),
        in_specs=[
            pl.BlockSpec((t, D_MODEL), lambda i: (i, 0)),
            pl.BlockSpec((t, D_MODEL), lambda i: (i, 0)),
            pl.BlockSpec((D_MODEL, D_MODEL), const),
            pl.BlockSpec((1, D_MODEL), const),
            pl.BlockSpec((N_EXPERTS, D_MODEL), const),
            pl.BlockSpec((N_EXPERTS, 1), const),
        ],
        out_specs=[
            pl.BlockSpec((t, D_MODEL), lambda i: (i, 0)),
            pl.BlockSpec((TOP_K, t), lambda i: (0, i)),
            pl.BlockSpec((8, t), lambda i: (0, i)),
            pl.BlockSpec((TOP_K, t), lambda i: (0, i)),
            pl.BlockSpec((N_EXPERTS, LANES), const),
        ],
        out_shape=[
            jax.ShapeDtypeStruct((n, D_MODEL), F32),
            jax.ShapeDtypeStruct((TOP_K, n), I32),
            jax.ShapeDtypeStruct((8, n), F32),
            jax.ShapeDtypeStruct((TOP_K, n), I32),
            jax.ShapeDtypeStruct((N_EXPERTS, LANES), F32),
        ],
        scratch_shapes=[pltpu.VMEM((N_EXPERTS, LANES), F32)],
        compiler_params=_cparams(("arbitrary",)),
        name="outproj",
    )(x2d, m, wo, g, wrt, br)


def _dispatch_body(dest_ref, x1_ref, g_ref, xs_ref, un_ref, sem):
    t = x1_ref.shape[0]
    un_ref[...] = _rms(x1_ref[...], g_ref[...])

    def row_copy(r, k):
        return pltpu.make_async_copy(un_ref.at[pl.ds(r, 1), :],
                                     xs_ref.at[pl.ds(dest_ref[k, r], 1), :], sem)

    def issue(r, carry):
        for k in range(TOP_K):
            row_copy(r, k).start()
        return carry

    lax.fori_loop(0, t, issue, 0)
    for k in range(TOP_K):
        pltpu.make_async_copy(un_ref, xs_ref.at[pl.ds(0, t), :], sem).wait()


def _dispatch(dest, x1, g, rows):
    n = x1.shape[0]
    t = _tile(n, 256)
    return pl.pallas_call(
        _dispatch_body,
        grid=(n // t,),
        in_specs=[
            pl.BlockSpec((TOP_K, t), lambda i: (0, i), memory_space=pltpu.SMEM),
            pl.BlockSpec((t, D_MODEL), lambda i: (i, 0)),
            pl.BlockSpec((1, D_MODEL), lambda i: (0, 0)),
        ],
        out_specs=pl.BlockSpec(memory_space=pl.ANY),
        out_shape=jax.ShapeDtypeStruct((rows, D_MODEL), F32),
        scratch_shapes=[pltpu.VMEM((t, D_MODEL), F32), pltpu.SemaphoreType.DMA(())],
        compiler_params=_cparams(("arbitrary",)),
        name="dispatch",
    )(dest, x1, g)


MOE_ROWS = 512
MOE_FF = 512


def _experts_body(be_ref, nv_ref, nu_ref, xs_ref, wg_ref, wu_ref, bg_ref, bu_ref, wd_ref, bd_ref,
                  ys_ref, xb_ref):
    i = pl.program_id(0)
    f = pl.program_id(1)

    @pl.when(i < nu_ref[0])
    def _():
        @pl.when(f == 0)
        def _():
            row = lax.broadcasted_iota(I32, xs_ref.shape, 0)
            xb_ref[...] = jnp.where(row < nv_ref[i], xs_ref[...], 0.0).astype(BF16)

        xb = xb_ref[...]
        g = jnp.dot(xb, wg_ref[0], preferred_element_type=F32) + bg_ref[0]
        u = jnp.dot(xb, wu_ref[0], preferred_element_type=F32) + bu_ref[0]
        g = jnp.minimum(g, SWIGLU_LIMIT)
        u = jnp.clip(u, -SWIGLU_LIMIT, SWIGLU_LIMIT)
        h = (g * _sigmoid(SWIGLU_ALPHA * g) * (u + 1.0)).astype(BF16)
        y = jnp.dot(h, wd_ref[0], preferred_element_type=F32)

        @pl.when(f == 0)
        def _():
            ys_ref[...] = y + bd_ref[0]

        @pl.when(f > 0)
        def _():
            ys_ref[...] = ys_ref[...] + y


def _experts(block_e, nvalid, n_used, xs, wg, wu, bg, bu, wd, bd):
    rows = xs.shape[0]
    tm = MOE_ROWS
    nb = rows // tm
    nf = D_FF // MOE_FF

    def blk(i, nu):
        return jnp.minimum(i, nu[0] - 1)

    def fidx(i, f, nu):
        return jnp.where(i < nu[0], f, nf - 1)

    grid_spec = pltpu.PrefetchScalarGridSpec(
        num_scalar_prefetch=3,
        grid=(nb, nf),
        in_specs=[
            pl.BlockSpec((tm, D_MODEL), lambda i, f, be, nv, nu: (blk(i, nu), 0)),
            pl.BlockSpec((1, D_MODEL, MOE_FF), lambda i, f, be, nv, nu: (be[blk(i, nu)], 0, fidx(i, f, nu))),
            pl.BlockSpec((1, D_MODEL, MOE_FF), lambda i, f, be, nv, nu: (be[blk(i, nu)], 0, fidx(i, f, nu))),
            pl.BlockSpec((1, 1, MOE_FF), lambda i, f, be, nv, nu: (be[blk(i, nu)], 0, fidx(i, f, nu))),
            pl.BlockSpec((1, 1, MOE_FF), lambda i, f, be, nv, nu: (be[blk(i, nu)], 0, fidx(i, f, nu))),
            pl.BlockSpec((1, MOE_FF, D_MODEL), lambda i, f, be, nv, nu: (be[blk(i, nu)], fidx(i, f, nu), 0)),
            pl.BlockSpec((1, 1, D_MODEL), lambda i, f, be, nv, nu: (be[blk(i, nu)], 0, 0)),
        ],
        out_specs=pl.BlockSpec((tm, D_MODEL), lambda i, f, be, nv, nu: (blk(i, nu), 0)),
        scratch_shapes=[pltpu.VMEM((tm, D_MODEL), BF16)],
    )
    return pl.pallas_call(
        _experts_body,
        grid_spec=grid_spec,
        out_shape=jax.ShapeDtypeStruct((rows, D_MODEL), F32),
        compiler_params=_cparams(("arbitrary", "arbitrary")),
        name="experts",
    )(block_e, nvalid, n_used, xs, wg, wu, bg, bu, wd, bd)


def _combine_body(dest_ref, x1_ref, gate_ref, g_ref, ys_ref, o_ref, buf_ref, sem):
    t = x1_ref.shape[0]

    def issue(r, carry):
        for k in range(TOP_K):
            pltpu.make_async_copy(ys_ref.at[pl.ds(dest_ref[k, r], 1), :],
                                  buf_ref.at[k, pl.ds(r, 1), :], sem).start()
        return carry

    lax.fori_loop(0, t, issue, 0)
    for k in range(TOP_K):
        pltpu.make_async_copy(ys_ref.at[pl.ds(0, t), :], buf_ref.at[k], sem).wait()
    gcol = gate_ref[...].T
    y = x1_ref[...]
    for k in range(TOP_K):
        y = y + gcol[:, k:k + 1] * buf_ref[k]
    o_ref[...] = _rms(y, g_ref[...])


def _combine(dest, x1, gate, g, ys):
    n = x1.shape[0]
    t = _tile(n, 256)
    return pl.pallas_call(
        _combine_body,
        grid=(n // t,),
        in_specs=[
            pl.BlockSpec((TOP_K, t), lambda i: (0, i), memory_space=pltpu.SMEM),
            pl.BlockSpec((t, D_MODEL), lambda i: (i, 0)),
            pl.BlockSpec((8, t), lambda i: (0, i)),
            pl.BlockSpec((1, D_MODEL), lambda i: (0, 0)),
            pl.BlockSpec(memory_space=pl.ANY),
        ],
        out_specs=pl.BlockSpec((t, D_MODEL), lambda i: (i, 0)),
        out_shape=jax.ShapeDtypeStruct((n, D_MODEL), F32),
        scratch_shapes=[pltpu.VMEM((TOP_K, t, D_MODEL), F32), pltpu.SemaphoreType.DMA(())],
        compiler_params=_cparams(("arbitrary",)),
        name="combine",
    )(dest, x1, gate, g, ys)


def _rot_cols(w):
    half = QK_ROPE // 2
    return jnp.concatenate([-w[..., half:], w[..., :half]], axis=-1)


def _prep_weights(norm_mix_g, w_in, conv_w, conv_b, rg_w_a, rg_b_a, rg_w_x, rg_b_x, rg_lambda,
                  w_proj_a, q_norm_g, w_uq, kv_norm_g, w_uk, w_uv, w_proj_b, w_out,
                  norm_ffn_g, w_router, b_router, w_gate_up, b_gate_up, w_down, b_down, norm_final_g):
    p = {}
    wi = w_in[0]
    o_xa, o_ga, o_cq, o_ckv, o_kr, o_gla, o_glb = 0, 1024, 2048, 2560, 3072, 3136, 5184
    p["w_main"] = jnp.concatenate([
        wi[:, o_gla:o_gla + D_MODEL], wi[:, o_glb:o_glb + D_MODEL],
        wi[:, o_xa:o_xa + D_RNN], wi[:, o_ga:o_ga + D_RNN],
        wi[:, o_cq:o_cq + Q_LORA], wi[:, o_ckv:o_ckv + KV_LORA]], axis=1).astype(BF16)
    wkr = wi[:, o_kr:o_kr + QK_ROPE]
    p["w_k2"] = jnp.concatenate([wkr, _rot_cols(wkr)], axis=1).astype(BF16)
    p["g_mix"] = norm_mix_g[0].reshape(1, D_MODEL)
    p["conv_w"] = conv_w[0]
    p["conv_b"] = conv_b[0].reshape(1, D_RNN)
    wa, wx = rg_w_a[0], rg_w_x[0]
    p["w4"] = jnp.concatenate([wa[0], wx[0], wa[1], wx[1]], axis=-1).astype(BF16)
    ba = rg_b_a[0].reshape(2, RNN_BLOCKS, 1, BLOCK_W)
    bx = rg_b_x[0].reshape(2, RNN_BLOCKS, 1, BLOCK_W)
    p["b4"] = jnp.concatenate([ba[0], bx[0], ba[1], bx[1]], axis=-1)
    lam = rg_lambda[0].reshape(2, RNN_BLOCKS, 1, BLOCK_W)
    p["lam4"] = jnp.concatenate([lam[0], lam[1]], axis=-1)
    p["w_a"] = w_proj_a[0].astype(BF16)
    p["q_g"] = q_norm_g[0].reshape(1, Q_LORA)
    p["kv_g"] = kv_norm_g[0].reshape(1, KV_LORA)
    wq = w_uq[0].reshape(Q_LORA, N_HEADS, QK_HEAD)
    p["wq_n"] = wq[:, :, :QK_NOPE].reshape(Q_LORA, N_HEADS * QK_NOPE).astype(BF16)
    wqr = wq[:, :, QK_NOPE:]
    p["wq_r"] = jnp.concatenate([wqr, _rot_cols(wqr)], axis=-1).reshape(Q_LORA, N_HEADS * LANES).astype(BF16)
    p["w_k"] = w_uk[0].astype(BF16)
    p["w_vt"] = w_uv[0].T.astype(BF16)
    p["w_b"] = w_proj_b[0].astype(BF16)
    p["w_o"] = w_out[0].astype(BF16)
    p["g_ffn"] = norm_ffn_g[0].reshape(1, D_MODEL)
    p["w_rt"] = w_router[0].T
    p["b_r"] = b_router[0].reshape(N_EXPERTS, 1)
    wgu = w_gate_up[0]
    p["w_g"] = wgu[:, :, 0::2].astype(BF16)
    p["w_u"] = wgu[:, :, 1::2].astype(BF16)
    bgu = b_gate_up[0]
    p["b_g"] = bgu[:, 0::2].reshape(N_EXPERTS, 1, D_FF)
    p["b_u"] = bgu[:, 1::2].reshape(N_EXPERTS, 1, D_FF)
    p["w_d"] = w_down[0].astype(BF16)
    p["b_d"] = b_down[0].reshape(N_EXPERTS, 1, D_MODEL)
    p["g_fin"] = norm_final_g.reshape(1, D_MODEL)
    return p


def _rope_tables(seq):
    half = QK_ROPE // 2
    freqs = ROPE_THETA ** (-jnp.arange(half, dtype=F32) / half)
    ang = jnp.arange(seq, dtype=F32)[:, None] * freqs[None, :]
    zero = jnp.zeros((seq, LANES - QK_ROPE), F32)
    ctab = jnp.concatenate([jnp.cos(ang), jnp.cos(ang), zero], axis=1)
    stab = jnp.concatenate([jnp.sin(ang), jnp.sin(ang), zero], axis=1)
    return ctab, stab


def _trunk(x, p):
    batch, seq, _ = x.shape
    n = batch * seq
    x2d = x.reshape(n, D_MODEL)
    z, zk = _in_proj(x2d, p["g_mix"], p["w_main"], p["w_k2"])
    hg = _rglru(z, p["conv_w"], p["conv_b"], p["w4"], p["b4"], p["lam4"], batch, seq)
    ctab, stab = _rope_tables(seq)
    qn, qr, kn, kr, vt = _qkv(z, zk, ctab, stab, p["q_g"], p["kv_g"],
                              p["wq_n"], p["wq_r"], p["w_k"], p["w_vt"], seq)
    attn = _attention(qn, qr, kn, kr, vt, batch, seq)
    m = _merge(hg, attn, z, p["w_a"], p["w_b"])
    x1, idx, gate, rank, cnt = _outproj(x2d, m, p["w_o"], p["g_ffn"], p["w_rt"], p["b_r"])

    tm = MOE_ROWS
    counts = cnt[:, 0].astype(I32)
    padded = (counts + tm - 1) // tm * tm
    pad_end = jnp.cumsum(padded)
    pad_start = pad_end - padded
    dest = pad_start[idx] + rank
    nb = -(-(n * TOP_K) // tm) + N_EXPERTS
    rows = nb * tm
    blk_start = jnp.arange(nb, dtype=I32) * tm
    block_e = jnp.minimum(jnp.searchsorted(pad_end, blk_start, side="right"), N_EXPERTS - 1).astype(I32)
    nvalid = jnp.clip(counts[block_e] - (blk_start - pad_start[block_e]), 0, tm).astype(I32)
    n_used = (pad_end[-1:] // tm).astype(I32)

    xs = _dispatch(dest, x1, p["g_ffn"], rows)
    ys = _experts(block_e, nvalid, n_used, xs, p["w_g"], p["w_u"], p["b_g"], p["b_u"], p["w_d"], p["b_d"])
    out = _combine(dest, x1, gate, p["g_fin"], ys)
    return out.reshape(batch, seq, D_MODEL)


def kernel(x_prompt, x_sample, norm_mix_g, w_in, conv_w, conv_b, rg_w_a, rg_b_a, rg_w_x, rg_b_x, rg_lambda,
           w_proj_a, q_norm_g, w_uq, kv_norm_g, w_uk, w_uv, w_proj_b, w_out, norm_ffn_g, w_router, b_router,
           w_gate_up, b_gate_up, w_down, b_down, norm_final_g):
    p = _prep_weights(norm_mix_g, w_in, conv_w, conv_b, rg_w_a, rg_b_a, rg_w_x, rg_b_x, rg_lambda,
                      w_proj_a, q_norm_g, w_uq, kv_norm_g, w_uk, w_uv, w_proj_b, w_out,
                      norm_ffn_g, w_router, b_router, w_gate_up, b_gate_up, w_down, b_down, norm_final_g)
    return (_trunk(x_prompt, p), _trunk(x_sample, p))
```

```python
import functools
import math

import jax
import jax.numpy as jnp
from jax import lax
from jax.experimental import pallas as pl
from jax.experimental.pallas import tpu as pltpu

F32 = jnp.float32
BF16 = jnp.bfloat16
I32 = jnp.int32

D_MODEL = 2048
D_RNN = 1024
RNN_BLOCKS = 8
BLOCK_W = 128
C_RG = 8.0
N_HEADS = 16
QK_NOPE = 128
QK_ROPE = 64
QK_HEAD = QK_NOPE + QK_ROPE
V_HEAD = 128
Q_LORA = 512
KV_LORA = 512
ROPE_THETA = 10000.0
N_EXPERTS = 32
TOP_K = 4
D_FF = 2048
SWIGLU_LIMIT = 7.0
SWIGLU_ALPHA = 1.702
EPS = 1e-6

LANES = 128
VMEM_LIMIT = 56 * 1024 * 1024
NEG_BIG = -1e30

Z_GLA, Z_GLB, Z_XA, Z_GA, Z_CQ, Z_CKV = 0, 2048, 4096, 5120, 6144, 6656
Z_COLS = 7168


def _cparams(sem):
    return pltpu.CompilerParams(dimension_semantics=sem, vmem_limit_bytes=VMEM_LIMIT)


def _tile(n, pref):
    t = min(n, pref)
    assert n % t == 0, (n, pref)
    return t


def _rms(x, g):
    return x * lax.rsqrt(jnp.mean(x * x, axis=-1, keepdims=True) + EPS) * g


def _in_proj_body(x_ref, g_ref, w_ref, wk_ref, z_ref, zk_ref, u_ref):
    @pl.when(pl.program_id(1) == 0)
    def _():
        u = _rms(x_ref[...], g_ref[...]).astype(BF16)
        u_ref[...] = u
        zk_ref[...] = jnp.dot(u, wk_ref[...], preferred_element_type=F32).astype(BF16)

    z_ref[...] = jnp.dot(u_ref[...], w_ref[...], preferred_element_type=F32).astype(BF16)


def _in_proj(x2d, g, w_main, w_k2):
    n = x2d.shape[0]
    tm = _tile(n, 1024)
    tn = 1024
    return pl.pallas_call(
        _in_proj_body,
        grid=(n // tm, Z_COLS // tn),
        in_specs=[
            pl.BlockSpec((tm, D_MODEL), lambda i, j: (i, 0)),
            pl.BlockSpec((1, D_MODEL), lambda i, j: (0, 0)),
            pl.BlockSpec((D_MODEL, tn), lambda i, j: (0, j)),
            pl.BlockSpec((D_MODEL, LANES), lambda i, j: (0, 0)),
        ],
        out_specs=[
            pl.BlockSpec((tm, tn), lambda i, j: (i, j)),
            pl.BlockSpec((tm, LANES), lambda i, j: (i, 0)),
        ],
        out_shape=[
            jax.ShapeDtypeStruct((n, Z_COLS), BF16),
            jax.ShapeDtypeStruct((n, LANES), BF16),
        ],
        scratch_shapes=[pltpu.VMEM((tm, D_MODEL), BF16)],
        compiler_params=_cparams(("arbitrary", "arbitrary")),
        name="in_proj",
    )(x2d, g, w_main, w_k2)


HALO = 16
SCAN_ROWS = 128


def _sigmoid(x):
    return 1.0 / (1.0 + jnp.exp(-x))


def _gelu_tanh(x):
    return 0.5 * x * (1.0 + jnp.tanh(math.sqrt(2.0 / math.pi) * (x + 0.044715 * (x * x * x))))


def _scan_chunk(a, b, reverse):
    rows = a.shape[0]
    row = lax.broadcasted_iota(I32, a.shape, 0)
    d = 1
    while d < rows:
        if d < 8:
            shift = (rows - d) if reverse else d
            a_sh = pltpu.roll(a, shift, axis=0)
            b_sh = pltpu.roll(b, shift, axis=0)
            keep = (row < rows - d) if reverse else (row >= d)
            a_sh = jnp.where(keep, a_sh, 1.0)
            b_sh = jnp.where(keep, b_sh, 0.0)
        else:
            one = jnp.ones((d, a.shape[1]), F32)
            zero = jnp.zeros((d, a.shape[1]), F32)
            if reverse:
                a_sh = jnp.concatenate([a[d:], one], axis=0)
                b_sh = jnp.concatenate([b[d:], zero], axis=0)
            else:
                a_sh = jnp.concatenate([one, a[:-d]], axis=0)
                b_sh = jnp.concatenate([zero, b[:-d]], axis=0)
        b = a * b_sh + b
        a = a * a_sh
        d *= 2
    return a, b


def _rglru_body(xa_ref, ga_ref, cw_ref, cb_ref, w4_ref, b4_ref, lam_ref, o_ref,
                a0_ref, b0_ref, a1_ref, b1_ref, *, seq, tc):
    nchunk = seq // tc
    lam = lam_ref[0]
    sp = jnp.maximum(-lam, 0.0) + jnp.log1p(jnp.exp(-jnp.abs(lam)))
    sp0 = sp[:, :BLOCK_W]
    sp1 = sp[:, BLOCK_W:]
    cw = cw_ref[...]
    cb = cb_ref[...]
    w4 = w4_ref[0]
    b4 = b4_ref[0]

    def gates(j, carry):
        r0 = pl.multiple_of(j * tc, tc)
        cur = xa_ref[pl.ds(r0, tc), :].astype(F32)
        prev_start = pl.multiple_of(jnp.maximum(r0 - HALO, 0), HALO)
        next_start = pl.multiple_of(jnp.minimum(r0 + tc, seq - HALO), HALO)
        prev = xa_ref[pl.ds(prev_start, HALO), :].astype(F32)
        nxt = xa_ref[pl.ds(next_start, HALO), :].astype(F32)
        prev = jnp.where(j > 0, prev, 0.0)
        nxt = jnp.where(j < nchunk - 1, nxt, 0.0)
        ext = jnp.concatenate([prev, cur, nxt], axis=0)
        n_ext = tc + 2 * HALO
        xm2 = pltpu.roll(ext, 2, axis=0)[HALO:HALO + tc]
        xm1 = pltpu.roll(ext, 1, axis=0)[HALO:HALO + tc]
        xp1 = pltpu.roll(ext, n_ext - 1, axis=0)[HALO:HALO + tc]
        xc = xm2 * cw[0:1] + xm1 * cw[1:2] + cur * cw[2:3] + xp1 * cw[3:4] + cb
        pre = jnp.dot(xc.astype(BF16), w4, preferred_element_type=F32) + b4
        for d, (a_ref, b_ref, spd) in enumerate(((a0_ref, b0_ref, sp0), (a1_ref, b1_ref, sp1))):
            r = _sigmoid(pre[:, (2 * d) * BLOCK_W:(2 * d + 1) * BLOCK_W])
            i = _sigmoid(pre[:, (2 * d + 1) * BLOCK_W:(2 * d + 2) * BLOCK_W])
            a = jnp.exp((-C_RG) * r * spd)
            b = jnp.sqrt(1.0 - a * a) * (i * xc)
            a_ref[pl.ds(r0, tc), :] = a
            b_ref[pl.ds(r0, tc), :] = b
        return carry

    lax.fori_loop(0, nchunk, gates, 0)

    nscan = seq // SCAN_ROWS

    def bwd(jj, h):
        j = nscan - 1 - jj
        r0 = pl.multiple_of(j * SCAN_ROWS, SCAN_ROWS)
        A, B = _scan_chunk(a1_ref[pl.ds(r0, SCAN_ROWS), :], b1_ref[pl.ds(r0, SCAN_ROWS), :], True)
        hh = A * h + B
        b1_ref[pl.ds(r0, SCAN_ROWS), :] = hh
        return hh[0:1, :]

    lax.fori_loop(0, nscan, bwd, jnp.zeros((1, BLOCK_W), F32))

    def fwd(j, h):
        r0 = pl.multiple_of(j * SCAN_ROWS, SCAN_ROWS)
        A, B = _scan_chunk(a0_ref[pl.ds(r0, SCAN_ROWS), :], b0_ref[pl.ds(r0, SCAN_ROWS), :], False)
        hh = A * h + B
        ga = ga_ref[pl.ds(r0, SCAN_ROWS), :].astype(F32)
        o_ref[pl.ds(r0, SCAN_ROWS), :] = ((hh + b1_ref[pl.ds(r0, SCAN_ROWS), :]) * _gelu_tanh(ga)).astype(BF16)
        return hh[SCAN_ROWS - 1:SCAN_ROWS, :]

    lax.fori_loop(0, nscan, fwd, jnp.zeros((1, BLOCK_W), F32))


def _rglru(z, conv_w, conv_b, w4, b4, lam4, batch, seq):
    n = batch * seq
    tc = _tile(seq, 256)
    xa_blk = Z_XA // BLOCK_W
    ga_blk = Z_GA // BLOCK_W
    return pl.pallas_call(
        functools.partial(_rglru_body, seq=seq, tc=tc),
        grid=(batch, RNN_BLOCKS),
        in_specs=[
            pl.BlockSpec((seq, BLOCK_W), lambda b, c: (b, xa_blk + c)),
            pl.BlockSpec((seq, BLOCK_W), lambda b, c: (b, ga_blk + c)),
            pl.BlockSpec((4, BLOCK_W), lambda b, c: (0, c)),
            pl.BlockSpec((1, BLOCK_W), lambda b, c: (0, c)),
            pl.BlockSpec((1, BLOCK_W, 4 * BLOCK_W), lambda b, c: (c, 0, 0)),
            pl.BlockSpec((1, 1, 4 * BLOCK_W), lambda b, c: (c, 0, 0)),
            pl.BlockSpec((1, 1, 2 * BLOCK_W), lambda b, c: (c, 0, 0)),
        ],
        out_specs=pl.BlockSpec((seq, BLOCK_W), lambda b, c: (b, c)),
        out_shape=jax.ShapeDtypeStruct((n, D_RNN), BF16),
        scratch_shapes=[pltpu.VMEM((seq, BLOCK_W), F32) for _ in range(4)],
        compiler_params=_cparams(("arbitrary", "arbitrary")),
        name="rglru",
    )(z, z, conv_w, conv_b, w4, b4, lam4)


KV_CHUNK = 512


def _qkv_body(cq_ref, ckv_ref, zk_ref, ct_ref, st_ref, qg_ref, kvg_ref,
              wqn_ref, wqr_ref, wk_ref, wvt_ref,
              qn_ref, qr_ref, kn_ref, kr_ref, vt_ref):
    scale = math.log2(math.e) / math.sqrt(QK_HEAD)
    ct = ct_ref[...]
    st = st_ref[...]
    cqn = _rms(cq_ref[...].astype(F32), qg_ref[...]).astype(BF16)
    ckvn = _rms(ckv_ref[...].astype(F32), kvg_ref[...]).astype(BF16)
    qn = jnp.dot(cqn, wqn_ref[...], preferred_element_type=F32)
    qn_ref[...] = (qn * scale).astype(BF16)
    qp = jnp.dot(cqn, wqr_ref[...], preferred_element_type=F32)
    cts = ct * scale
    sts = st * scale
    for h in range(N_HEADS):
        p = qp[:, h * LANES:(h + 1) * LANES]
        qr_ref[:, h * LANES:(h + 1) * LANES] = (p * cts + pltpu.roll(p, LANES // 2, axis=1) * sts).astype(BF16)
    kn_ref[...] = jnp.dot(ckvn, wk_ref[...], preferred_element_type=F32).astype(BF16)
    zk = zk_ref[...].astype(F32)
    kr_ref[...] = (zk * ct + pltpu.roll(zk, LANES // 2, axis=1) * st).astype(BF16)
    vt = lax.dot_general(wvt_ref[...], ckvn, (((1,), (1,)), ((), ())), preferred_element_type=F32)
    vt_ref[0] = vt.astype(BF16)


def _qkv(z, zk, ctab, stab, qg, kvg, wqn, wqr, wk, wvt, seq):
    n = z.shape[0]
    t = _tile(seq, KV_CHUNK)
    spt = seq // t
    cq_blk = Z_CQ // Q_LORA
    ckv_blk = Z_CKV // KV_LORA
    hd = N_HEADS * LANES
    const = lambda i: (0, 0)
    return pl.pallas_call(
        _qkv_body,
        grid=(n // t,),
        in_specs=[
            pl.BlockSpec((t, Q_LORA), lambda i: (i, cq_blk)),
            pl.BlockSpec((t, KV_LORA), lambda i: (i, ckv_blk)),
            pl.BlockSpec((t, LANES), lambda i: (i, 0)),
            pl.BlockSpec((t, LANES), lambda i: (i % spt, 0)),
            pl.BlockSpec((t, LANES), lambda i: (i % spt, 0)),
            pl.BlockSpec((1, Q_LORA), const),
            pl.BlockSpec((1, KV_LORA), const),
            pl.BlockSpec((Q_LORA, hd), const),
            pl.BlockSpec((Q_LORA, hd), const),
            pl.BlockSpec((KV_LORA, hd), const),
            pl.BlockSpec((hd, KV_LORA), const),
        ],
        out_specs=[
            pl.BlockSpec((t, hd), lambda i: (i, 0)),
            pl.BlockSpec((t, hd), lambda i: (i, 0)),
            pl.BlockSpec((t, hd), lambda i: (i, 0)),
            pl.BlockSpec((t, LANES), lambda i: (i, 0)),
            pl.BlockSpec((1, hd, t), lambda i: (i, 0, 0)),
        ],
        out_shape=[
            jax.ShapeDtypeStruct((n, hd), BF16),
            jax.ShapeDtypeStruct((n, hd), BF16),
            jax.ShapeDtypeStruct((n, hd), BF16),
            jax.ShapeDtypeStruct((n, LANES), BF16),
            jax.ShapeDtypeStruct((n // t, hd, t), BF16),
        ],
        compiler_params=_cparams(("arbitrary",)),
        name="qkv",
    )(z, z, zk, ctab, stab, qg, kvg, wqn, wqr, wk, wvt)


ATTN_SPLIT = 2


def _attn_body(qn_ref, qr_ref, kn_ref, kr_ref, vt_ref, o_ref, *, nk, kc):
    tq = qn_ref.shape[0]
    q = jnp.concatenate([qn_ref[...], qr_ref[...]], axis=1)

    def scores(c):
        k = jnp.concatenate([kn_ref[c * kc:(c + 1) * kc, :], kr_ref[c * kc:(c + 1) * kc, :]], axis=1)
        return lax.dot_general(k, q, (((1,), (1,)), ((), ())), preferred_element_type=F32)

    m = jnp.full((1, tq), NEG_BIG, F32)
    l = jnp.zeros((1, tq), F32)
    acc = jnp.zeros((V_HEAD, tq), F32)
    s_next = scores(0)
    for c in range(nk):
        s = s_next
        if c + 1 < nk:
            s_next = scores(c + 1)
        m_new = jnp.maximum(m, jnp.max(s, axis=0, keepdims=True))
        alpha = jnp.exp2(m - m_new)
        p = jnp.exp2(s - m_new)
        l = alpha * l + jnp.sum(p, axis=0, keepdims=True)
        acc = alpha * acc + jnp.dot(vt_ref[c], p.astype(BF16), preferred_element_type=F32)
        m = m_new
    o = acc * (1.0 / l)
    o_ref[...] = o.T.astype(BF16)


def _attention(qn, qr, kn, kr, vt, batch, seq):
    n = batch * seq
    tq = _tile(seq, 512)
    kc = _tile(seq, KV_CHUNK)
    nq = seq // tq
    nk = seq // kc
    return pl.pallas_call(
        functools.partial(_attn_body, nk=nk, kc=kc),
        grid=(batch, N_HEADS, nq),
        in_specs=[
            pl.BlockSpec((tq, LANES), lambda b, h, i: (b * nq + i, h)),
            pl.BlockSpec((tq, LANES), lambda b, h, i: (b * nq + i, h)),
            pl.BlockSpec((seq, LANES), lambda b, h, i: (b, h)),
            pl.BlockSpec((seq, LANES), lambda b, h, i: (b, 0)),
            pl.BlockSpec((nk, V_HEAD, kc), lambda b, h, i: (b, h, 0)),
        ],
        out_specs=pl.BlockSpec((tq, V_HEAD), lambda b, h, i: (b * nq + i, h)),
        out_shape=jax.ShapeDtypeStruct((n, N_HEADS * V_HEAD), BF16),
        compiler_params=_cparams(("arbitrary", "arbitrary", "arbitrary")),
        name="attn",
    )(qn, qr, kn, kr, vt)


def _merge_body(hg_ref, at_ref, gla_ref, glb_ref, wa_ref, wb_ref, m_ref):
    ya = jnp.dot(hg_ref[...], wa_ref[...], preferred_element_type=F32)
    yb = jnp.dot(at_ref[...], wb_ref[...], preferred_element_type=F32)
    m = _sigmoid(gla_ref[...].astype(F32)) * ya + _sigmoid(glb_ref[...].astype(F32)) * yb
    m_ref[...] = m.astype(BF16)


def _merge(hg, attn, z, wa, wb):
    n = hg.shape[0]
    t = _tile(n, 256)
    const = lambda i: (0, 0)
    return pl.pallas_call(
        _merge_body,
        grid=(n // t,),
        in_specs=[
            pl.BlockSpec((t, D_RNN), lambda i: (i, 0)),
            pl.BlockSpec((t, D_MODEL), lambda i: (i, 0)),
            pl.BlockSpec((t, D_MODEL), lambda i: (i, Z_GLA // D_MODEL)),
            pl.BlockSpec((t, D_MODEL), lambda i: (i, Z_GLB // D_MODEL)),
            pl.BlockSpec((D_RNN, D_MODEL), const),
            pl.BlockSpec((D_MODEL, D_MODEL), const),
        ],
        out_specs=pl.BlockSpec((t, D_MODEL), lambda i: (i, 0)),
        out_shape=jax.ShapeDtypeStruct((n, D_MODEL), BF16),
        compiler_params=_cparams(("arbitrary",)),
        name="merge",
    )(hg, attn, z, z, wa, wb)


def _outproj_body(x_ref, m_ref, wo_ref, g_ref, wrt_ref, br_ref,
                  x1_ref, idx_ref, gate_ref, rank_ref, cnt_ref, carry_ref):
    t = x_ref.shape[0]

    @pl.when(pl.program_id(0) == 0)
    def _():
        carry_ref[...] = jnp.zeros(carry_ref.shape, F32)

    x1 = x_ref[...] + jnp.dot(m_ref[...], wo_ref[...], preferred_element_type=F32)
    x1_ref[...] = x1
    un = _rms(x1, g_ref[...])
    logits = lax.dot_general(wrt_ref[...], un, (((1,), (1,)), ((), ())),
                             precision=lax.Precision.HIGHEST,
                             preferred_element_type=F32) + br_ref[...]
    e_iota = lax.broadcasted_iota(I32, logits.shape, 0)
    vals, idxs, sels = [], [], []
    cur = logits
    for _ in range(TOP_K):
        mx = jnp.max(cur, axis=0, keepdims=True)
        idx = jnp.min(jnp.where(cur == mx, e_iota, N_EXPERTS), axis=0, keepdims=True)
        sel = e_iota == idx
        vals.append(mx)
        idxs.append(idx)
        sels.append(sel)
        cur = jnp.where(sel, -jnp.inf, cur)
    ex = [jnp.exp(v - vals[0]) for v in vals]
    inv = 1.0 / (ex[0] + ex[1] + ex[2] + ex[3])
    onehot = jnp.where(sels[0] | sels[1] | sels[2] | sels[3], 1.0, 0.0)
    row = lax.broadcasted_iota(I32, (t, t), 0)
    col = lax.broadcasted_iota(I32, (t, t), 1)
    upper = jnp.where(row < col, 1.0, 0.0).astype(BF16)
    prefix = jnp.dot(onehot.astype(BF16), upper, preferred_element_type=F32) + carry_ref[:, 0:1]
    for k in range(TOP_K):
        idx_ref[k:k + 1, :] = idxs[k]
        gate_ref[k:k + 1, :] = ex[k] * inv
        rank_ref[k:k + 1, :] = jnp.sum(jnp.where(sels[k], prefix, 0.0), axis=0, keepdims=True).astype(I32)
    gate_ref[TOP_K:, :] = jnp.zeros((gate_ref.shape[0] - TOP_K, t), F32)
    carry_ref[...] = carry_ref[...] + jnp.sum(onehot, axis=1, keepdims=True)
    cnt_ref[...] = carry_ref[...]


def _outproj(x2d, m, wo, g, wrt, br):
    n = x2d.shape[0]
    t = _tile(n, 256)
    const = lambda i: (0, 0)
    return pl.pallas_call(
        _outproj_body,
        grid=(n // t,),
        in_specs=[
            pl.BlockSpec((t, D_MODEL), lambda i: (i, 0)),
            pl.BlockSpec((t, D_MODEL), lambda i: (i, 0)),
            pl.BlockSpec((D_MODEL, D_MODEL), const),
            pl.BlockSpec((1, D_MODEL), const),
            pl.BlockSpec((N_EXPERTS, D_MODEL), const),
            pl.BlockSpec((N_EXPERTS, 1), const),
        ],
        out_specs=[
            pl.BlockSpec((t, D_MODEL), lambda i: (i, 0)),
            pl.BlockSpec((TOP_K, t), lambda i: (0, i)),
            pl.BlockSpec((8, t), lambda i: (0, i)),
            pl.BlockSpec((TOP_K, t), lambda i: (0, i)),
            pl.BlockSpec((N_EXPERTS, LANES), const),
        ],
        out_shape=[
            jax.ShapeDtypeStruct((n, D_MODEL), F32),
            jax.ShapeDtypeStruct((TOP_K, n), I32),
            jax.ShapeDtypeStruct((8, n), F32),
            jax.ShapeDtypeStruct((TOP_K, n), I32),
            jax.ShapeDtypeStruct((N_EXPERTS, LANES), F32),
        ],
        scratch_shapes=[pltpu.VMEM((N_EXPERTS, LANES), F32)],
        compiler_params=_cparams(("arbitrary",)),
        name="outproj",
    )(x2d, m, wo, g, wrt, br)


ZERO_ROWS = 256


def _dispatch_body(fs_ref, fe_ref, nu_ref, dest_ref, x1_ref, g_ref, xs_ref, un_ref, zero_ref, sem, zsem,
                   *, nsteps, nblocks):
    t = x1_ref.shape[0]
    i = pl.program_id(0)
    slot = i % 2

    def wait_slot(s):
        for _ in range(TOP_K):
            pltpu.make_async_copy(un_ref.at[s], xs_ref.at[pl.ds(0, t), :], sem.at[s]).wait()

    @pl.when(i >= 2)
    def _():
        wait_slot(slot)

    un_ref[slot] = _rms(x1_ref[...], g_ref[...])

    def issue(r, carry):
        for k in range(TOP_K):
            pltpu.make_async_copy(un_ref.at[slot, pl.ds(r, 1), :],
                                  xs_ref.at[pl.ds(dest_ref[k, r], 1), :], sem.at[slot]).start()
        return carry

    lax.fori_loop(0, t, issue, 0)

    @pl.when(i == 0)
    def _():
        zero_ref[...] = jnp.zeros(zero_ref.shape, F32)

        def fill_expert(e, carry):
            def start(r, c):
                pltpu.make_async_copy(zero_ref.at[pl.ds(0, 1), :], xs_ref.at[pl.ds(r, 1), :], zsem).start()
                return c

            def wait(r, c):
                pltpu.make_async_copy(zero_ref.at[pl.ds(0, 1), :], xs_ref.at[pl.ds(0, 1), :], zsem).wait()
                return c

            lax.fori_loop(fs_ref[e], fe_ref[e], start, 0)
            lax.fori_loop(fs_ref[e], fe_ref[e], wait, 0)
            return carry

        lax.fori_loop(0, N_EXPERTS, fill_expert, 0)

        def fill_block(b, carry):
            for part in range(MOE_ROWS // ZERO_ROWS):
                r0 = pl.multiple_of(b * MOE_ROWS + part * ZERO_ROWS, ZERO_ROWS)
                cp = pltpu.make_async_copy(zero_ref, xs_ref.at[pl.ds(r0, ZERO_ROWS), :], zsem)
                cp.start()
                cp.wait()
            return carry

        lax.fori_loop(nu_ref[0], nblocks, fill_block, 0)

    @pl.when(i == nsteps - 1)
    def _():
        wait_slot(slot)
        if nsteps >= 2:
            wait_slot(1 - slot)


def _dispatch(fill_start, fill_end, n_used, dest, x1, g, rows):
    n = x1.shape[0]
    t = _tile(n, 256)
    nsteps = n // t
    grid_spec = pltpu.PrefetchScalarGridSpec(
        num_scalar_prefetch=3,
        grid=(nsteps,),
        in_specs=[
            pl.BlockSpec((TOP_K, t), lambda i, fs, fe, nu: (0, i), memory_space=pltpu.SMEM),
            pl.BlockSpec((t, D_MODEL), lambda i, fs, fe, nu: (i, 0)),
            pl.BlockSpec((1, D_MODEL), lambda i, fs, fe, nu: (0, 0)),
        ],
        out_specs=pl.BlockSpec(memory_space=pl.ANY),
        scratch_shapes=[pltpu.VMEM((2, t, D_MODEL), F32), pltpu.VMEM((ZERO_ROWS, D_MODEL), F32),
                        pltpu.SemaphoreType.DMA((2,)), pltpu.SemaphoreType.DMA(())],
    )
    return pl.pallas_call(
        functools.partial(_dispatch_body, nsteps=nsteps, nblocks=rows // MOE_ROWS),
        grid_spec=grid_spec,
        out_shape=jax.ShapeDtypeStruct((rows, D_MODEL), F32),
        compiler_params=_cparams(("arbitrary",)),
        name="dispatch",
    )(fill_start, fill_end, n_used, dest, x1, g)


MOE_ROWS = 512
MOE_FF = 512


def _experts_body(be_ref, nu_ref, xs_ref, wg_ref, wu_ref, bg_ref, bu_ref, wd_ref, bd_ref,
                  ys_ref, xb_ref):
    i = pl.program_id(0)
    f = pl.program_id(1)

    @pl.when(jnp.logical_and(i >= nu_ref[0], f == 0))
    def _():
        ys_ref[...] = jnp.zeros(ys_ref.shape, F32)

    @pl.when(i < nu_ref[0])
    def _():
        @pl.when(f == 0)
        def _():
            xb_ref[...] = xs_ref[...].astype(BF16)

        xb = xb_ref[...]
        nt = (((1,), (1,)), ((), ()))
        g = lax.dot_general(xb, wg_ref[0], nt, preferred_element_type=F32) + bg_ref[0]
        u = lax.dot_general(xb, wu_ref[0], nt, preferred_element_type=F32) + bu_ref[0]
        g = jnp.minimum(g, SWIGLU_LIMIT)
        u = jnp.clip(u, -SWIGLU_LIMIT, SWIGLU_LIMIT)
        h = (g * _sigmoid(SWIGLU_ALPHA * g) * (u + 1.0)).astype(BF16)
        y = jnp.dot(h, wd_ref[0], preferred_element_type=F32)

        @pl.when(f == 0)
        def _():
            ys_ref[...] = y + bd_ref[0]

        @pl.when(f > 0)
        def _():
            ys_ref[...] = ys_ref[...] + y


def _experts(block_e, n_used, xs, wgu_t, bg, bu, wd, bd):
    rows = xs.shape[0]
    tm = MOE_ROWS
    nb = rows // tm
    nf = D_FF // MOE_FF

    def blk(i, nu):
        return jnp.minimum(i, nu[0] - 1)

    def fidx(i, f, nu):
        return jnp.where(i < nu[0], f, nf - 1)

    grid_spec = pltpu.PrefetchScalarGridSpec(
        num_scalar_prefetch=2,
        grid=(nb, nf),
        in_specs=[
            pl.BlockSpec((tm, D_MODEL), lambda i, f, be, nu: (blk(i, nu), 0)),
            pl.BlockSpec((1, MOE_FF, D_MODEL), lambda i, f, be, nu: (be[blk(i, nu)], fidx(i, f, nu), 0)),
            pl.BlockSpec((1, MOE_FF, D_MODEL), lambda i, f, be, nu: (be[blk(i, nu)], fidx(i, f, nu), 1)),
            pl.BlockSpec((1, 1, MOE_FF), lambda i, f, be, nu: (be[blk(i, nu)], 0, fidx(i, f, nu))),
            pl.BlockSpec((1, 1, MOE_FF), lambda i, f, be, nu: (be[blk(i, nu)], 0, fidx(i, f, nu))),
            pl.BlockSpec((1, MOE_FF, D_MODEL), lambda i, f, be, nu: (be[blk(i, nu)], fidx(i, f, nu), 0)),
            pl.BlockSpec((1, 1, D_MODEL), lambda i, f, be, nu: (be[blk(i, nu)], 0, 0)),
        ],
        out_specs=pl.BlockSpec((tm, D_MODEL), lambda i, f, be, nu: (i, 0)),
        scratch_shapes=[pltpu.VMEM((tm, D_MODEL), BF16)],
    )
    return pl.pallas_call(
        _experts_body,
        grid_spec=grid_spec,
        out_shape=jax.ShapeDtypeStruct((rows, D_MODEL), F32),
        compiler_params=_cparams(("arbitrary", "arbitrary")),
        name="experts",
    )(block_e, n_used, xs, wgu_t, wgu_t, bg, bu, wd, bd)


def _combine_body(dest_ref, x1_ref, gate_ref, g_ref, ys_ref, o_ref, buf_ref, sem):
    t = x1_ref.shape[0]

    def issue(r, carry):
        for k in range(TOP_K):
            pltpu.make_async_copy(ys_ref.at[pl.ds(dest_ref[k, r], 1), :],
                                  buf_ref.at[k, pl.ds(r, 1), :], sem).start()
        return carry

    lax.fori_loop(0, t, issue, 0)
    for k in range(TOP_K):
        pltpu.make_async_copy(ys_ref.at[pl.ds(0, t), :], buf_ref.at[k], sem).wait()
    gcol = gate_ref[...].T
    y = x1_ref[...]
    for k in range(TOP_K):
        y = y + gcol[:, k:k + 1] * buf_ref[k]
    o_ref[...] = _rms(y, g_ref[...])


def _combine(dest, x1, gate, g, ys):
    n = x1.shape[0]
    t = _tile(n, 256)
    return pl.pallas_call(
        _combine_body,
        grid=(n // t,),
        in_specs=[
            pl.BlockSpec((TOP_K, t), lambda i: (0, i), memory_space=pltpu.SMEM),
            pl.BlockSpec((t, D_MODEL), lambda i: (i, 0)),
            pl.BlockSpec((8, t), lambda i: (0, i)),
            pl.BlockSpec((1, D_MODEL), lambda i: (0, 0)),
            pl.BlockSpec(memory_space=pl.ANY),
        ],
        out_specs=pl.BlockSpec((t, D_MODEL), lambda i: (i, 0)),
        out_shape=jax.ShapeDtypeStruct((n, D_MODEL), F32),
        scratch_shapes=[pltpu.VMEM((TOP_K, t, D_MODEL), F32), pltpu.SemaphoreType.DMA(())],
        compiler_params=_cparams(("arbitrary",)),
        name="combine",
    )(dest, x1, gate, g, ys)


def _rot_cols(w):
    half = QK_ROPE // 2
    return jnp.concatenate([-w[..., half:], w[..., :half]], axis=-1)


def _prep_weights(norm_mix_g, w_in, conv_w, conv_b, rg_w_a, rg_b_a, rg_w_x, rg_b_x, rg_lambda,
                  w_proj_a, q_norm_g, w_uq, kv_norm_g, w_uk, w_uv, w_proj_b, w_out,
                  norm_ffn_g, w_router, b_router, w_gate_up, b_gate_up, w_down, b_down, norm_final_g):
    p = {}
    wi = w_in[0]
    o_xa, o_ga, o_cq, o_ckv, o_kr, o_gla, o_glb = 0, 1024, 2048, 2560, 3072, 3136, 5184
    p["w_main"] = jnp.concatenate([
        wi[:, o_gla:o_gla + D_MODEL], wi[:, o_glb:o_glb + D_MODEL],
        wi[:, o_xa:o_xa + D_RNN], wi[:, o_ga:o_ga + D_RNN],
        wi[:, o_cq:o_cq + Q_LORA], wi[:, o_ckv:o_ckv + KV_LORA]], axis=1).astype(BF16)
    wkr = wi[:, o_kr:o_kr + QK_ROPE]
    p["w_k2"] = jnp.concatenate([wkr, _rot_cols(wkr)], axis=1).astype(BF16)
    p["g_mix"] = norm_mix_g[0].reshape(1, D_MODEL)
    p["conv_w"] = conv_w[0]
    p["conv_b"] = conv_b[0].reshape(1, D_RNN)
    wa, wx = rg_w_a[0], rg_w_x[0]
    p["w4"] = jnp.concatenate([wa[0], wx[0], wa[1], wx[1]], axis=-1).astype(BF16)
    ba = rg_b_a[0].reshape(2, RNN_BLOCKS, 1, BLOCK_W)
    bx = rg_b_x[0].reshape(2, RNN_BLOCKS, 1, BLOCK_W)
    p["b4"] = jnp.concatenate([ba[0], bx[0], ba[1], bx[1]], axis=-1)
    lam = rg_lambda[0].reshape(2, RNN_BLOCKS, 1, BLOCK_W)
    p["lam4"] = jnp.concatenate([lam[0], lam[1]], axis=-1)
    p["w_a"] = w_proj_a[0].astype(BF16)
    p["q_g"] = q_norm_g[0].reshape(1, Q_LORA)
    p["kv_g"] = kv_norm_g[0].reshape(1, KV_LORA)
    wq = w_uq[0].reshape(Q_LORA, N_HEADS, QK_HEAD)
    p["wq_n"] = wq[:, :, :QK_NOPE].reshape(Q_LORA, N_HEADS * QK_NOPE).astype(BF16)
    wqr = wq[:, :, QK_NOPE:]
    p["wq_r"] = jnp.concatenate([wqr, _rot_cols(wqr)], axis=-1).reshape(Q_LORA, N_HEADS * LANES).astype(BF16)
    p["w_k"] = w_uk[0].astype(BF16)
    p["w_vt"] = w_uv[0].T.astype(BF16)
    p["w_b"] = w_proj_b[0].astype(BF16)
    p["w_o"] = w_out[0].astype(BF16)
    p["g_ffn"] = norm_ffn_g[0].reshape(1, D_MODEL)
    p["w_rt"] = w_router[0].T
    p["b_r"] = b_router[0].reshape(N_EXPERTS, 1)
    wgu_t = jnp.transpose(w_gate_up[0], (0, 2, 1)).astype(BF16)
    p["w_gu_t"] = wgu_t.reshape(N_EXPERTS, D_FF, 2 * D_MODEL)
    bgu = b_gate_up[0]
    p["b_g"] = bgu[:, 0::2].reshape(N_EXPERTS, 1, D_FF)
    p["b_u"] = bgu[:, 1::2].reshape(N_EXPERTS, 1, D_FF)
    p["w_d"] = w_down[0].astype(BF16)
    p["b_d"] = b_down[0].reshape(N_EXPERTS, 1, D_MODEL)
    p["g_fin"] = norm_final_g.reshape(1, D_MODEL)
    return p


def _rope_tables(seq):
    half = QK_ROPE // 2
    freqs = ROPE_THETA ** (-jnp.arange(half, dtype=F32) / half)
    ang = jnp.arange(seq, dtype=F32)[:, None] * freqs[None, :]
    zero = jnp.zeros((seq, LANES - QK_ROPE), F32)
    ctab = jnp.concatenate([jnp.cos(ang), jnp.cos(ang), zero], axis=1)
    stab = jnp.concatenate([jnp.sin(ang), jnp.sin(ang), zero], axis=1)
    return ctab, stab


def _trunk(x, p):
    batch, seq, _ = x.shape
    n = batch * seq
    x2d = x.reshape(n, D_MODEL)
    z, zk = _in_proj(x2d, p["g_mix"], p["w_main"], p["w_k2"])
    hg = _rglru(z, p["conv_w"], p["conv_b"], p["w4"], p["b4"], p["lam4"], batch, seq)
    ctab, stab = _rope_tables(seq)
    qn, qr, kn, kr, vt = _qkv(z, zk, ctab, stab, p["q_g"], p["kv_g"],
                              p["wq_n"], p["wq_r"], p["w_k"], p["w_vt"], seq)
    attn = _attention(qn, qr, kn, kr, vt, batch, seq)
    m = _merge(hg, attn, z, p["w_a"], p["w_b"])
    x1, idx, gate, rank, cnt = _outproj(x2d, m, p["w_o"], p["g_ffn"], p["w_rt"], p["b_r"])

    tm = MOE_ROWS
    counts = cnt[:, 0].astype(I32)
    padded = (counts + tm - 1) // tm * tm
    pad_end = jnp.cumsum(padded)
    pad_start = pad_end - padded
    experts = jnp.arange(N_EXPERTS, dtype=I32)
    dest = jnp.sum(jnp.where(idx[:, :, None] == experts, pad_start, 0), axis=-1) + rank
    nb = -(-(n * TOP_K) // tm) + N_EXPERTS
    rows = nb * tm
    blk_start = jnp.arange(nb, dtype=I32) * tm
    block_e = jnp.minimum(jnp.sum((pad_end[None, :] <= blk_start[:, None]).astype(I32), axis=1), N_EXPERTS - 1)
    n_used = (pad_end[-1:] // tm).astype(I32)

    xs = _dispatch(pad_start + counts, pad_end, n_used, dest, x1, p["g_ffn"], rows)
    ys = _experts(block_e, n_used, xs, p["w_gu_t"], p["b_g"], p["b_u"], p["w_d"], p["b_d"])
    out = _combine(dest, x1, gate, p["g_fin"], ys)
    return out.reshape(batch, seq, D_MODEL)


def kernel(x_prompt, x_sample, norm_mix_g, w_in, conv_w, conv_b, rg_w_a, rg_b_a, rg_w_x, rg_b_x, rg_lambda,
           w_proj_a, q_norm_g, w_uq, kv_norm_g, w_uk, w_uv, w_proj_b, w_out, norm_ffn_g, w_router, b_router,
           w_gate_up, b_gate_up, w_down, b_down, norm_final_g):
    p = _prep_weights(norm_mix_g, w_in, conv_w, conv_b, rg_w_a, rg_b_a, rg_w_x, rg_b_x, rg_lambda,
                      w_proj_a, q_norm_g, w_uq, kv_norm_g, w_uk, w_uv, w_proj_b, w_out,
                      norm_ffn_g, w_router, b_router, w_gate_up, b_gate_up, w_down, b_down, norm_final_g)
    return (_trunk(x_prompt, p), _trunk(x_sample, p))
```

```python
import functools
import math

import jax
import jax.numpy as jnp
from jax import lax
from jax.experimental import pallas as pl
from jax.experimental.pallas import tpu as pltpu

F32 = jnp.float32
BF16 = jnp.bfloat16
I32 = jnp.int32

D_MODEL = 2048
D_RNN = 1024
RNN_BLOCKS = 8
BLOCK_W = 128
C_RG = 8.0
N_HEADS = 16
QK_NOPE = 128
QK_ROPE = 64
QK_HEAD = QK_NOPE + QK_ROPE
V_HEAD = 128
Q_LORA = 512
KV_LORA = 512
ROPE_THETA = 10000.0
N_EXPERTS = 32
TOP_K = 4
D_FF = 2048
SWIGLU_LIMIT = 7.0
SWIGLU_ALPHA = 1.702
EPS = 1e-6

LANES = 128
VMEM_LIMIT = 56 * 1024 * 1024
NEG_BIG = -1e30

Z_GLA, Z_GLB, Z_XA, Z_GA, Z_CQ, Z_CKV = 0, 2048, 4096, 5120, 6144, 6656
Z_COLS = 7168


def _cparams(sem):
    return pltpu.CompilerParams(dimension_semantics=sem, vmem_limit_bytes=VMEM_LIMIT)


def _tile(n, pref):
    t = min(n, pref)
    assert n % t == 0, (n, pref)
    return t


def _rms(x, g):
    return x * lax.rsqrt(jnp.mean(x * x, axis=-1, keepdims=True) + EPS) * g


def _in_proj_body(x_ref, g_ref, w_ref, wk_ref, z_ref, zk_ref, u_ref):
    @pl.when(pl.program_id(1) == 0)
    def _():
        u = _rms(x_ref[...], g_ref[...]).astype(BF16)
        u_ref[...] = u
        zk_ref[...] = jnp.dot(u, wk_ref[...], preferred_element_type=F32).astype(BF16)

    z_ref[...] = jnp.dot(u_ref[...], w_ref[...], preferred_element_type=F32).astype(BF16)


def _in_proj(x2d, g, w_main, w_k2):
    n = x2d.shape[0]
    tm = _tile(n, 1024)
    tn = 1024
    return pl.pallas_call(
        _in_proj_body,
        grid=(n // tm, Z_COLS // tn),
        in_specs=[
            pl.BlockSpec((tm, D_MODEL), lambda i, j: (i, 0)),
            pl.BlockSpec((1, D_MODEL), lambda i, j: (0, 0)),
            pl.BlockSpec((D_MODEL, tn), lambda i, j: (0, j)),
            pl.BlockSpec((D_MODEL, LANES), lambda i, j: (0, 0)),
        ],
        out_specs=[
            pl.BlockSpec((tm, tn), lambda i, j: (i, j)),
            pl.BlockSpec((tm, LANES), lambda i, j: (i, 0)),
        ],
        out_shape=[
            jax.ShapeDtypeStruct((n, Z_COLS), BF16),
            jax.ShapeDtypeStruct((n, LANES), BF16),
        ],
        scratch_shapes=[pltpu.VMEM((tm, D_MODEL), BF16)],
        compiler_params=_cparams(("arbitrary", "arbitrary")),
        name="in_proj",
    )(x2d, g, w_main, w_k2)


HALO = 16
SCAN_ROWS = 128


def _sigmoid(x):
    return 1.0 / (1.0 + jnp.exp(-x))


def _gelu_tanh(x):
    return 0.5 * x * (1.0 + jnp.tanh(math.sqrt(2.0 / math.pi) * (x + 0.044715 * (x * x * x))))


def _scan_chunk(a, b, reverse):
    rows = a.shape[0]
    row = lax.broadcasted_iota(I32, a.shape, 0)
    d = 1
    while d < rows:
        if d < 8:
            shift = (rows - d) if reverse else d
            a_sh = pltpu.roll(a, shift, axis=0)
            b_sh = pltpu.roll(b, shift, axis=0)
            keep = (row < rows - d) if reverse else (row >= d)
            a_sh = jnp.where(keep, a_sh, 1.0)
            b_sh = jnp.where(keep, b_sh, 0.0)
        else:
            one = jnp.ones((d, a.shape[1]), F32)
            zero = jnp.zeros((d, a.shape[1]), F32)
            if reverse:
                a_sh = jnp.concatenate([a[d:], one], axis=0)
                b_sh = jnp.concatenate([b[d:], zero], axis=0)
            else:
                a_sh = jnp.concatenate([one, a[:-d]], axis=0)
                b_sh = jnp.concatenate([zero, b[:-d]], axis=0)
        b = a * b_sh + b
        a = a * a_sh
        d *= 2
    return a, b


def _rglru_body(xa_ref, ga_ref, cw_ref, cb_ref, w4_ref, b4_ref, lam_ref, o_ref,
                a0_ref, b0_ref, a1_ref, b1_ref, *, seq, tc):
    nchunk = seq // tc
    lam = lam_ref[0]
    sp = jnp.maximum(-lam, 0.0) + jnp.log1p(jnp.exp(-jnp.abs(lam)))
    sp0 = sp[:, :BLOCK_W]
    sp1 = sp[:, BLOCK_W:]
    cw = cw_ref[...]
    cb = cb_ref[...]
    w4 = w4_ref[0]
    b4 = b4_ref[0]

    def gates(j, carry):
        r0 = pl.multiple_of(j * tc, tc)
        cur = xa_ref[pl.ds(r0, tc), :].astype(F32)
        prev_start = pl.multiple_of(jnp.maximum(r0 - HALO, 0), HALO)
        next_start = pl.multiple_of(jnp.minimum(r0 + tc, seq - HALO), HALO)
        prev = xa_ref[pl.ds(prev_start, HALO), :].astype(F32)
        nxt = xa_ref[pl.ds(next_start, HALO), :].astype(F32)
        prev = jnp.where(j > 0, prev, 0.0)
        nxt = jnp.where(j < nchunk - 1, nxt, 0.0)
        ext = jnp.concatenate([prev, cur, nxt], axis=0)
        n_ext = tc + 2 * HALO
        xm2 = pltpu.roll(ext, 2, axis=0)[HALO:HALO + tc]
        xm1 = pltpu.roll(ext, 1, axis=0)[HALO:HALO + tc]
        xp1 = pltpu.roll(ext, n_ext - 1, axis=0)[HALO:HALO + tc]
        xc = xm2 * cw[0:1] + xm1 * cw[1:2] + cur * cw[2:3] + xp1 * cw[3:4] + cb
        pre = jnp.dot(xc.astype(BF16), w4, preferred_element_type=F32) + b4
        for d, (a_ref, b_ref, spd) in enumerate(((a0_ref, b0_ref, sp0), (a1_ref, b1_ref, sp1))):
            r = _sigmoid(pre[:, (2 * d) * BLOCK_W:(2 * d + 1) * BLOCK_W])
            i = _sigmoid(pre[:, (2 * d + 1) * BLOCK_W:(2 * d + 2) * BLOCK_W])
            a = jnp.exp((-C_RG) * r * spd)
            b = jnp.sqrt(1.0 - a * a) * (i * xc)
            a_ref[pl.ds(r0, tc), :] = a
            b_ref[pl.ds(r0, tc), :] = b
        return carry

    lax.fori_loop(0, nchunk, gates, 0)

    nscan = seq // SCAN_ROWS

    def bwd(jj, h):
        j = nscan - 1 - jj
        r0 = pl.multiple_of(j * SCAN_ROWS, SCAN_ROWS)
        A, B = _scan_chunk(a1_ref[pl.ds(r0, SCAN_ROWS), :], b1_ref[pl.ds(r0, SCAN_ROWS), :], True)
        hh = A * h + B
        b1_ref[pl.ds(r0, SCAN_ROWS), :] = hh
        return hh[0:1, :]

    lax.fori_loop(0, nscan, bwd, jnp.zeros((1, BLOCK_W), F32))

    def fwd(j, h):
        r0 = pl.multiple_of(j * SCAN_ROWS, SCAN_ROWS)
        A, B = _scan_chunk(a0_ref[pl.ds(r0, SCAN_ROWS), :], b0_ref[pl.ds(r0, SCAN_ROWS), :], False)
        hh = A * h + B
        ga = ga_ref[pl.ds(r0, SCAN_ROWS), :].astype(F32)
        o_ref[pl.ds(r0, SCAN_ROWS), :] = ((hh + b1_ref[pl.ds(r0, SCAN_ROWS), :]) * _gelu_tanh(ga)).astype(BF16)
        return hh[SCAN_ROWS - 1:SCAN_ROWS, :]

    lax.fori_loop(0, nscan, fwd, jnp.zeros((1, BLOCK_W), F32))


def _rglru(z, conv_w, conv_b, w4, b4, lam4, batch, seq):
    n = batch * seq
    tc = _tile(seq, 256)
    xa_blk = Z_XA // BLOCK_W
    ga_blk = Z_GA // BLOCK_W
    return pl.pallas_call(
        functools.partial(_rglru_body, seq=seq, tc=tc),
        grid=(batch, RNN_BLOCKS),
        in_specs=[
            pl.BlockSpec((seq, BLOCK_W), lambda b, c: (b, xa_blk + c)),
            pl.BlockSpec((seq, BLOCK_W), lambda b, c: (b, ga_blk + c)),
            pl.BlockSpec((4, BLOCK_W), lambda b, c: (0, c)),
            pl.BlockSpec((1, BLOCK_W), lambda b, c: (0, c)),
            pl.BlockSpec((1, BLOCK_W, 4 * BLOCK_W), lambda b, c: (c, 0, 0)),
            pl.BlockSpec((1, 1, 4 * BLOCK_W), lambda b, c: (c, 0, 0)),
            pl.BlockSpec((1, 1, 2 * BLOCK_W), lambda b, c: (c, 0, 0)),
        ],
        out_specs=pl.BlockSpec((seq, BLOCK_W), lambda b, c: (b, c)),
        out_shape=jax.ShapeDtypeStruct((n, D_RNN), BF16),
        scratch_shapes=[pltpu.VMEM((seq, BLOCK_W), F32) for _ in range(4)],
        compiler_params=_cparams(("arbitrary", "arbitrary")),
        name="rglru",
    )(z, z, conv_w, conv_b, w4, b4, lam4)


KV_CHUNK = 512


def _qkv_body(cq_ref, ckv_ref, zk_ref, ct_ref, st_ref, qg_ref, kvg_ref,
              wqn_ref, wqr_ref, wk_ref, wvt_ref,
              qn_ref, qr_ref, kn_ref, kr_ref, vt_ref):
    scale = math.log2(math.e) / math.sqrt(QK_HEAD)
    ct = ct_ref[...]
    st = st_ref[...]
    cqn = _rms(cq_ref[...].astype(F32), qg_ref[...]).astype(BF16)
    ckvn = _rms(ckv_ref[...].astype(F32), kvg_ref[...]).astype(BF16)
    qn = jnp.dot(cqn, wqn_ref[...], preferred_element_type=F32)
    qn_ref[...] = (qn * scale).astype(BF16)
    qp = jnp.dot(cqn, wqr_ref[...], preferred_element_type=F32)
    cts = ct * scale
    sts = st * scale
    for h in range(N_HEADS):
        p = qp[:, h * LANES:(h + 1) * LANES]
        qr_ref[:, h * LANES:(h + 1) * LANES] = (p * cts + pltpu.roll(p, LANES // 2, axis=1) * sts).astype(BF16)
    kn_ref[...] = jnp.dot(ckvn, wk_ref[...], preferred_element_type=F32).astype(BF16)
    zk = zk_ref[...].astype(F32)
    kr_ref[...] = (zk * ct + pltpu.roll(zk, LANES // 2, axis=1) * st).astype(BF16)
    vt = lax.dot_general(wvt_ref[...], ckvn, (((1,), (1,)), ((), ())), preferred_element_type=F32)
    vt_ref[0] = vt.astype(BF16)


def _qkv(z, zk, ctab, stab, qg, kvg, wqn, wqr, wk, wvt, seq):
    n = z.shape[0]
    t = _tile(seq, KV_CHUNK)
    spt = seq // t
    cq_blk = Z_CQ // Q_LORA
    ckv_blk = Z_CKV // KV_LORA
    hd = N_HEADS * LANES
    const = lambda i: (0, 0)
    return pl.pallas_call(
        _qkv_body,
        grid=(n // t,),
        in_specs=[
            pl.BlockSpec((t, Q_LORA), lambda i: (i, cq_blk)),
            pl.BlockSpec((t, KV_LORA), lambda i: (i, ckv_blk)),
            pl.BlockSpec((t, LANES), lambda i: (i, 0)),
            pl.BlockSpec((t, LANES), lambda i: (i % spt, 0)),
            pl.BlockSpec((t, LANES), lambda i: (i % spt, 0)),
            pl.BlockSpec((1, Q_LORA), const),
            pl.BlockSpec((1, KV_LORA), const),
            pl.BlockSpec((Q_LORA, hd), const),
            pl.BlockSpec((Q_LORA, hd), const),
            pl.BlockSpec((KV_LORA, hd), const),
            pl.BlockSpec((hd, KV_LORA), const),
        ],
        out_specs=[
            pl.BlockSpec((t, hd), lambda i: (i, 0)),
            pl.BlockSpec((t, hd), lambda i: (i, 0)),
            pl.BlockSpec((t, hd), lambda i: (i, 0)),
            pl.BlockSpec((t, LANES), lambda i: (i, 0)),
            pl.BlockSpec((1, hd, t), lambda i: (i, 0, 0)),
        ],
        out_shape=[
            jax.ShapeDtypeStruct((n, hd), BF16),
            jax.ShapeDtypeStruct((n, hd), BF16),
            jax.ShapeDtypeStruct((n, hd), BF16),
            jax.ShapeDtypeStruct((n, LANES), BF16),
            jax.ShapeDtypeStruct((n // t, hd, t), BF16),
        ],
        compiler_params=_cparams(("arbitrary",)),
        name="qkv",
    )(z, z, zk, ctab, stab, qg, kvg, wqn, wqr, wk, wvt)


ATTN_CHAINS = 2
ONES_PAD = 16


def _attn_body(qn_ref, qr_ref, kn_ref, kr_ref, vt_ref, o_ref, *, nk, kc):
    tq = qn_ref.shape[0]
    th = tq // ATTN_CHAINS
    qs = [jnp.concatenate([qn_ref[h * th:(h + 1) * th, :], qr_ref[h * th:(h + 1) * th, :]], axis=1)
          for h in range(ATTN_CHAINS)]

    def keys(c):
        return jnp.concatenate([kn_ref[c * kc:(c + 1) * kc, :], kr_ref[c * kc:(c + 1) * kc, :]], axis=1)

    def scores(k, h):
        return lax.dot_general(k, qs[h], (((1,), (1,)), ((), ())), preferred_element_type=F32)

    ones_rows = jnp.where(lax.broadcasted_iota(I32, (ONES_PAD, kc), 0) == 0, 1.0, 0.0).astype(BF16)
    m = [jnp.full((1, th), NEG_BIG, F32) for _ in range(ATTN_CHAINS)]
    acc = [jnp.zeros((V_HEAD + ONES_PAD, th), F32) for _ in range(ATTN_CHAINS)]
    k_first = keys(0)
    s_next = [scores(k_first, h) for h in range(ATTN_CHAINS)]
    for c in range(nk):
        s_cur = s_next
        if c + 1 < nk:
            k_ahead = keys(c + 1)
            s_next = [scores(k_ahead, h) for h in range(ATTN_CHAINS)]
        v_ext = jnp.concatenate([vt_ref[c], ones_rows], axis=0)
        for h in range(ATTN_CHAINS):
            s = s_cur[h]
            m_new = jnp.maximum(m[h], jnp.max(s, axis=0, keepdims=True))
            alpha = jnp.exp2(m[h] - m_new)
            p = jnp.exp2((s - m_new).astype(BF16))
            acc[h] = alpha * acc[h] + jnp.dot(v_ext, p, preferred_element_type=F32)
            m[h] = m_new
    for h in range(ATTN_CHAINS):
        o = acc[h][:V_HEAD] * (1.0 / acc[h][V_HEAD:V_HEAD + 1])
        o_ref[h * th:(h + 1) * th, :] = o.T.astype(BF16)


def _attention(qn, qr, kn, kr, vt, batch, seq):
    n = batch * seq
    tq = _tile(seq, 1024)
    kc = _tile(seq, KV_CHUNK)
    nq = seq // tq
    nk = seq // kc
    return pl.pallas_call(
        functools.partial(_attn_body, nk=nk, kc=kc),
        grid=(batch, N_HEADS, nq),
        in_specs=[
            pl.BlockSpec((tq, LANES), lambda b, h, i: (b * nq + i, h)),
            pl.BlockSpec((tq, LANES), lambda b, h, i: (b * nq + i, h)),
            pl.BlockSpec((seq, LANES), lambda b, h, i: (b, h)),
            pl.BlockSpec((seq, LANES), lambda b, h, i: (b, 0)),
            pl.BlockSpec((nk, V_HEAD, kc), lambda b, h, i: (b, h, 0)),
        ],
        out_specs=pl.BlockSpec((tq, V_HEAD), lambda b, h, i: (b * nq + i, h)),
        out_shape=jax.ShapeDtypeStruct((n, N_HEADS * V_HEAD), BF16),
        compiler_params=_cparams(("arbitrary", "arbitrary", "arbitrary")),
        name="attn",
    )(qn, qr, kn, kr, vt)


def _merge_body(hg_ref, at_ref, gla_ref, glb_ref, wa_ref, wb_ref, m_ref):
    ya = jnp.dot(hg_ref[...], wa_ref[...], preferred_element_type=F32)
    yb = jnp.dot(at_ref[...], wb_ref[...], preferred_element_type=F32)
    m = _sigmoid(gla_ref[...].astype(F32)) * ya + _sigmoid(glb_ref[...].astype(F32)) * yb
    m_ref[...] = m.astype(BF16)


def _merge(hg, attn, z, wa, wb):
    n = hg.shape[0]
    t = _tile(n, 256)
    const = lambda i: (0, 0)
    return pl.pallas_call(
        _merge_body,
        grid=(n // t,),
        in_specs=[
            pl.BlockSpec((t, D_RNN), lambda i: (i, 0)),
            pl.BlockSpec((t, D_MODEL), lambda i: (i, 0)),
            pl.BlockSpec((t, D_MODEL), lambda i: (i, Z_GLA // D_MODEL)),
            pl.BlockSpec((t, D_MODEL), lambda i: (i, Z_GLB // D_MODEL)),
            pl.BlockSpec((D_RNN, D_MODEL), const),
            pl.BlockSpec((D_MODEL, D_MODEL), const),
        ],
        out_specs=pl.BlockSpec((t, D_MODEL), lambda i: (i, 0)),
        out_shape=jax.ShapeDtypeStruct((n, D_MODEL), BF16),
        compiler_params=_cparams(("arbitrary",)),
        name="merge",
    )(hg, attn, z, z, wa, wb)


def _outproj_body(x_ref, m_ref, wo_ref, g_ref, wr_ref, br_ref,
                  x1_ref, idx_ref, gate_ref, rank_ref, cnt_ref, carry_ref):
    t = x_ref.shape[0]

    @pl.when(pl.program_id(0) == 0)
    def _():
        carry_ref[...] = jnp.zeros(carry_ref.shape, F32)

    x1 = x_ref[...] + jnp.dot(m_ref[...], wo_ref[...], preferred_element_type=F32)
    x1_ref[...] = x1
    un = _rms(x1, g_ref[...])
    un_hi = un.astype(BF16)
    un_lo = (un - un_hi.astype(F32)).astype(BF16)
    w_hi = wr_ref[0]
    logits = (jnp.dot(un_hi, w_hi, preferred_element_type=F32)
              + jnp.dot(un_lo, w_hi, preferred_element_type=F32)
              + jnp.dot(un_hi, wr_ref[1], preferred_element_type=F32)).T + br_ref[...]
    e_iota = lax.broadcasted_iota(I32, logits.shape, 0)
    vals, idxs, sels = [], [], []
    cur = logits
    for _ in range(TOP_K):
        mx = jnp.max(cur, axis=0, keepdims=True)
        idx = jnp.min(jnp.where(cur == mx, e_iota, N_EXPERTS), axis=0, keepdims=True)
        sel = e_iota == idx
        vals.append(mx)
        idxs.append(idx)
        sels.append(sel)
        cur = jnp.where(sel, -jnp.inf, cur)
    ex = [jnp.exp(v - vals[0]) for v in vals]
    inv = 1.0 / (ex[0] + ex[1] + ex[2] + ex[3])
    onehot = jnp.where(sels[0] | sels[1] | sels[2] | sels[3], 1.0, 0.0)
    row = lax.broadcasted_iota(I32, (t, t), 0)
    col = lax.broadcasted_iota(I32, (t, t), 1)
    upper = jnp.where(row < col, 1.0, 0.0).astype(BF16)
    prefix = jnp.dot(onehot.astype(BF16), upper, preferred_element_type=F32) + carry_ref[:, 0:1]
    for k in range(TOP_K):
        idx_ref[k:k + 1, :] = idxs[k]
        gate_ref[k:k + 1, :] = ex[k] * inv
        rank_ref[k:k + 1, :] = jnp.sum(jnp.where(sels[k], prefix, 0.0), axis=0, keepdims=True).astype(I32)
    gate_ref[TOP_K:, :] = jnp.zeros((gate_ref.shape[0] - TOP_K, t), F32)
    carry_ref[...] = carry_ref[...] + jnp.sum(onehot, axis=1, keepdims=True)
    cnt_ref[...] = carry_ref[...]


def _outproj(x2d, m, wo, g, wr2, br):
    n = x2d.shape[0]
    t = _tile(n, 256)
    const = lambda i: (0, 0)
    return pl.pallas_call(
        _outproj_body,
        grid=(n // t,),
        in_specs=[
            pl.BlockSpec((t, D_MODEL), lambda i: (i, 0)),
            pl.BlockSpec((t, D_MODEL), lambda i: (i, 0)),
            pl.BlockSpec((D_MODEL, D_MODEL), const),
            pl.BlockSpec((1, D_MODEL), const),
            pl.BlockSpec((2, D_MODEL, N_EXPERTS), lambda i: (0, 0, 0)),
            pl.BlockSpec((N_EXPERTS, 1), const),
        ],
        out_specs=[
            pl.BlockSpec((t, D_MODEL), lambda i: (i, 0)),
            pl.BlockSpec((TOP_K, t), lambda i: (0, i)),
            pl.BlockSpec((8, t), lambda i: (0, i)),
            pl.BlockSpec((TOP_K, t), lambda i: (0, i)),
            pl.BlockSpec((N_EXPERTS, LANES), const),
        ],
        out_shape=[
            jax.ShapeDtypeStruct((n, D_MODEL), F32),
            jax.ShapeDtypeStruct((TOP_K, n), I32),
            jax.ShapeDtypeStruct((8, n), F32),
            jax.ShapeDtypeStruct((TOP_K, n), I32),
            jax.ShapeDtypeStruct((N_EXPERTS, LANES), F32),
        ],
        scratch_shapes=[pltpu.VMEM((N_EXPERTS, LANES), F32)],
        compiler_params=_cparams(("arbitrary",)),
        name="outproj",
    )(x2d, m, wo, g, wr2, br)


ZERO_ROWS = 256


def _dispatch_body(fs_ref, fe_ref, nu_ref, dest_ref, x1_ref, g_ref, xs_ref, un_ref, zero_ref, sem, zsem,
                   *, nsteps, nblocks):
    t = x1_ref.shape[0]
    i = pl.program_id(0)
    slot = i % 2

    def wait_slot(s):
        for _ in range(TOP_K):
            pltpu.make_async_copy(un_ref.at[s], xs_ref.at[pl.ds(0, t), :], sem.at[s]).wait()

    @pl.when(i >= 2)
    def _():
        wait_slot(slot)

    un_ref[slot] = _rms(x1_ref[...], g_ref[...])

    def issue(r, carry):
        for k in range(TOP_K):
            pltpu.make_async_copy(un_ref.at[slot, pl.ds(r, 1), :],
                                  xs_ref.at[pl.ds(dest_ref[k, r], 1), :], sem.at[slot]).start()
        return carry

    lax.fori_loop(0, t, issue, 0, unroll=8)

    @pl.when(i == 0)
    def _():
        zero_ref[...] = jnp.zeros(zero_ref.shape, F32)

        def fill_expert(e, carry):
            def start(r, c):
                pltpu.make_async_copy(zero_ref.at[pl.ds(0, 1), :], xs_ref.at[pl.ds(r, 1), :], zsem).start()
                return c

            def wait(r, c):
                pltpu.make_async_copy(zero_ref.at[pl.ds(0, 1), :], xs_ref.at[pl.ds(0, 1), :], zsem).wait()
                return c

            lax.fori_loop(fs_ref[e], fe_ref[e], start, 0)
            lax.fori_loop(fs_ref[e], fe_ref[e], wait, 0)
            return carry

        lax.fori_loop(0, N_EXPERTS, fill_expert, 0)

        def fill_block(b, carry):
            for part in range(MOE_ROWS // ZERO_ROWS):
                r0 = pl.multiple_of(b * MOE_ROWS + part * ZERO_ROWS, ZERO_ROWS)
                cp = pltpu.make_async_copy(zero_ref, xs_ref.at[pl.ds(r0, ZERO_ROWS), :], zsem)
                cp.start()
                cp.wait()
            return carry

        lax.fori_loop(nu_ref[0], nblocks, fill_block, 0)

    @pl.when(i == nsteps - 1)
    def _():
        wait_slot(slot)
        if nsteps >= 2:
            wait_slot(1 - slot)


def _dispatch(fill_start, fill_end, n_used, dest, x1, g, rows):
    n = x1.shape[0]
    t = _tile(n, 256)
    nsteps = n // t
    grid_spec = pltpu.PrefetchScalarGridSpec(
        num_scalar_prefetch=3,
        grid=(nsteps,),
        in_specs=[
            pl.BlockSpec((TOP_K, t), lambda i, fs, fe, nu: (0, i), memory_space=pltpu.SMEM),
            pl.BlockSpec((t, D_MODEL), lambda i, fs, fe, nu: (i, 0)),
            pl.BlockSpec((1, D_MODEL), lambda i, fs, fe, nu: (0, 0)),
        ],
        out_specs=pl.BlockSpec(memory_space=pl.ANY),
        scratch_shapes=[pltpu.VMEM((2, t, D_MODEL), F32), pltpu.VMEM((ZERO_ROWS, D_MODEL), F32),
                        pltpu.SemaphoreType.DMA((2,)), pltpu.SemaphoreType.DMA(())],
    )
    return pl.pallas_call(
        functools.partial(_dispatch_body, nsteps=nsteps, nblocks=rows // MOE_ROWS),
        grid_spec=grid_spec,
        out_shape=jax.ShapeDtypeStruct((rows, D_MODEL), F32),
        compiler_params=_cparams(("arbitrary",)),
        name="dispatch",
    )(fill_start, fill_end, n_used, dest, x1, g)


MOE_ROWS = 512
MOE_FF = 1024


def _experts_body(be_ref, nu_ref, xs_ref, wg_ref, wu_ref, bg_ref, bu_ref, wd_ref, bd_ref,
                  ys_ref, xb_ref):
    i = pl.program_id(0)
    f = pl.program_id(1)

    @pl.when(jnp.logical_and(i >= nu_ref[0], f == 0))
    def _():
        ys_ref[...] = jnp.zeros(ys_ref.shape, F32)

    @pl.when(i < nu_ref[0])
    def _():
        @pl.when(f == 0)
        def _():
            xb_ref[...] = xs_ref[...].astype(BF16)

        xb = xb_ref[...]
        nt = (((1,), (1,)), ((), ()))
        g = lax.dot_general(xb, wg_ref[0], nt, preferred_element_type=F32) + bg_ref[0]
        u = lax.dot_general(xb, wu_ref[0], nt, preferred_element_type=F32) + bu_ref[0]
        g = jnp.minimum(g, SWIGLU_LIMIT)
        u = jnp.clip(u, -SWIGLU_LIMIT, SWIGLU_LIMIT)
        h = (g * _sigmoid(SWIGLU_ALPHA * g) * (u + 1.0)).astype(BF16)
        y = jnp.dot(h, wd_ref[0], preferred_element_type=F32)

        @pl.when(f == 0)
        def _():
            ys_ref[...] = y + bd_ref[0]

        @pl.when(f > 0)
        def _():
            ys_ref[...] = ys_ref[...] + y


def _experts(block_e, n_used, xs, wgu_t, bg, bu, wd, bd):
    rows = xs.shape[0]
    tm = MOE_ROWS
    nb = rows // tm
    nf = D_FF // MOE_FF

    def blk(i, nu):
        return jnp.minimum(i, nu[0] - 1)

    def fidx(i, f, nu):
        return jnp.where(i < nu[0], f, nf - 1)

    grid_spec = pltpu.PrefetchScalarGridSpec(
        num_scalar_prefetch=2,
        grid=(nb, nf),
        in_specs=[
            pl.BlockSpec((tm, D_MODEL), lambda i, f, be, nu: (blk(i, nu), 0)),
            pl.BlockSpec((1, MOE_FF, D_MODEL), lambda i, f, be, nu: (be[blk(i, nu)], fidx(i, f, nu), 0)),
            pl.BlockSpec((1, MOE_FF, D_MODEL), lambda i, f, be, nu: (be[blk(i, nu)], fidx(i, f, nu), 1)),
            pl.BlockSpec((1, 1, MOE_FF), lambda i, f, be, nu: (be[blk(i, nu)], 0, fidx(i, f, nu))),
            pl.BlockSpec((1, 1, MOE_FF), lambda i, f, be, nu: (be[blk(i, nu)], 0, fidx(i, f, nu))),
            pl.BlockSpec((1, MOE_FF, D_MODEL), lambda i, f, be, nu: (be[blk(i, nu)], fidx(i, f, nu), 0)),
            pl.BlockSpec((1, 1, D_MODEL), lambda i, f, be, nu: (be[blk(i, nu)], 0, 0)),
        ],
        out_specs=pl.BlockSpec((tm, D_MODEL), lambda i, f, be, nu: (i, 0)),
        scratch_shapes=[pltpu.VMEM((tm, D_MODEL), BF16)],
    )
    return pl.pallas_call(
        _experts_body,
        grid_spec=grid_spec,
        out_shape=jax.ShapeDtypeStruct((rows, D_MODEL), F32),
        compiler_params=_cparams(("arbitrary", "arbitrary")),
        name="experts",
    )(block_e, n_used, xs, wgu_t, wgu_t, bg, bu, wd, bd)


def _combine_body(dest_ref, x1_ref, gate_ref, g_ref, ys_ref, o_ref, buf_ref, sem):
    t = x1_ref.shape[0]

    def issue(r, carry):
        for k in range(TOP_K):
            pltpu.make_async_copy(ys_ref.at[pl.ds(dest_ref[k, r], 1), :],
                                  buf_ref.at[k, pl.ds(r, 1), :], sem).start()
        return carry

    lax.fori_loop(0, t, issue, 0, unroll=8)
    for k in range(TOP_K):
        pltpu.make_async_copy(ys_ref.at[pl.ds(0, t), :], buf_ref.at[k], sem).wait()
    gcol = gate_ref[...].T
    y = x1_ref[...]
    for k in range(TOP_K):
        y = y + gcol[:, k:k + 1] * buf_ref[k]
    o_ref[...] = _rms(y, g_ref[...])


def _combine(dest, x1, gate, g, ys):
    n = x1.shape[0]
    t = _tile(n, 256)
    return pl.pallas_call(
        _combine_body,
        grid=(n // t,),
        in_specs=[
            pl.BlockSpec((TOP_K, t), lambda i: (0, i), memory_space=pltpu.SMEM),
            pl.BlockSpec((t, D_MODEL), lambda i: (i, 0)),
            pl.BlockSpec((8, t), lambda i: (0, i)),
            pl.BlockSpec((1, D_MODEL), lambda i: (0, 0)),
            pl.BlockSpec(memory_space=pl.ANY),
        ],
        out_specs=pl.BlockSpec((t, D_MODEL), lambda i: (i, 0)),
        out_shape=jax.ShapeDtypeStruct((n, D_MODEL), F32),
        scratch_shapes=[pltpu.VMEM((TOP_K, t, D_MODEL), F32), pltpu.SemaphoreType.DMA(())],
        compiler_params=_cparams(("arbitrary",)),
        name="combine",
    )(dest, x1, gate, g, ys)


def _rot_cols(w):
    half = QK_ROPE // 2
    return jnp.concatenate([-w[..., half:], w[..., :half]], axis=-1)


def _prep_weights(norm_mix_g, w_in, conv_w, conv_b, rg_w_a, rg_b_a, rg_w_x, rg_b_x, rg_lambda,
                  w_proj_a, q_norm_g, w_uq, kv_norm_g, w_uk, w_uv, w_proj_b, w_out,
                  norm_ffn_g, w_router, b_router, w_gate_up, b_gate_up, w_down, b_down, norm_final_g):
    p = {}
    wi = w_in[0]
    o_xa, o_ga, o_cq, o_ckv, o_kr, o_gla, o_glb = 0, 1024, 2048, 2560, 3072, 3136, 5184
    p["w_main"] = jnp.concatenate([
        wi[:, o_gla:o_gla + D_MODEL], wi[:, o_glb:o_glb + D_MODEL],
        wi[:, o_xa:o_xa + D_RNN], wi[:, o_ga:o_ga + D_RNN],
        wi[:, o_cq:o_cq + Q_LORA], wi[:, o_ckv:o_ckv + KV_LORA]], axis=1).astype(BF16)
    wkr = wi[:, o_kr:o_kr + QK_ROPE]
    p["w_k2"] = jnp.concatenate([wkr, _rot_cols(wkr)], axis=1).astype(BF16)
    p["g_mix"] = norm_mix_g[0].reshape(1, D_MODEL)
    p["conv_w"] = conv_w[0]
    p["conv_b"] = conv_b[0].reshape(1, D_RNN)
    wa, wx = rg_w_a[0], rg_w_x[0]
    p["w4"] = jnp.concatenate([wa[0], wx[0], wa[1], wx[1]], axis=-1).astype(BF16)
    ba = rg_b_a[0].reshape(2, RNN_BLOCKS, 1, BLOCK_W)
    bx = rg_b_x[0].reshape(2, RNN_BLOCKS, 1, BLOCK_W)
    p["b4"] = jnp.concatenate([ba[0], bx[0], ba[1], bx[1]], axis=-1)
    lam = rg_lambda[0].reshape(2, RNN_BLOCKS, 1, BLOCK_W)
    p["lam4"] = jnp.concatenate([lam[0], lam[1]], axis=-1)
    p["w_a"] = w_proj_a[0].astype(BF16)
    p["q_g"] = q_norm_g[0].reshape(1, Q_LORA)
    p["kv_g"] = kv_norm_g[0].reshape(1, KV_LORA)
    wq = w_uq[0].reshape(Q_LORA, N_HEADS, QK_HEAD)
    p["wq_n"] = wq[:, :, :QK_NOPE].reshape(Q_LORA, N_HEADS * QK_NOPE).astype(BF16)
    wqr = wq[:, :, QK_NOPE:]
    p["wq_r"] = jnp.concatenate([wqr, _rot_cols(wqr)], axis=-1).reshape(Q_LORA, N_HEADS * LANES).astype(BF16)
    p["w_k"] = w_uk[0].astype(BF16)
    p["w_vt"] = w_uv[0].T.astype(BF16)
    p["w_b"] = w_proj_b[0].astype(BF16)
    p["w_o"] = w_out[0].astype(BF16)
    p["g_ffn"] = norm_ffn_g[0].reshape(1, D_MODEL)
    w_r_hi = w_router[0].astype(BF16)
    w_r_lo = (w_router[0] - w_r_hi.astype(F32)).astype(BF16)
    p["w_r2"] = jnp.stack([w_r_hi, w_r_lo])
    p["b_r"] = b_router[0].reshape(N_EXPERTS, 1)
    wgu_t = jnp.transpose(w_gate_up[0], (0, 2, 1)).astype(BF16)
    p["w_gu_t"] = wgu_t.reshape(N_EXPERTS, D_FF, 2 * D_MODEL)
    bgu = b_gate_up[0]
    p["b_g"] = bgu[:, 0::2].reshape(N_EXPERTS, 1, D_FF)
    p["b_u"] = bgu[:, 1::2].reshape(N_EXPERTS, 1, D_FF)
    p["w_d"] = w_down[0].astype(BF16)
    p["b_d"] = b_down[0].reshape(N_EXPERTS, 1, D_MODEL)
    p["g_fin"] = norm_final_g.reshape(1, D_MODEL)
    return p


def _rope_tables(seq):
    half = QK_ROPE // 2
    freqs = ROPE_THETA ** (-jnp.arange(half, dtype=F32) / half)
    ang = jnp.arange(seq, dtype=F32)[:, None] * freqs[None, :]
    zero = jnp.zeros((seq, LANES - QK_ROPE), F32)
    ctab = jnp.concatenate([jnp.cos(ang), jnp.cos(ang), zero], axis=1)
    stab = jnp.concatenate([jnp.sin(ang), jnp.sin(ang), zero], axis=1)
    return ctab, stab


def _trunk(x, p):
    batch, seq, _ = x.shape
    n = batch * seq
    x2d = x.reshape(n, D_MODEL)
    z, zk = _in_proj(x2d, p["g_mix"], p["w_main"], p["w_k2"])
    hg = _rglru(z, p["conv_w"], p["conv_b"], p["w4"], p["b4"], p["lam4"], batch, seq)
    ctab, stab = _rope_tables(seq)
    qn, qr, kn, kr, vt = _qkv(z, zk, ctab, stab, p["q_g"], p["kv_g"],
                              p["wq_n"], p["wq_r"], p["w_k"], p["w_vt"], seq)
    attn = _attention(qn, qr, kn, kr, vt, batch, seq)
    m = _merge(hg, attn, z, p["w_a"], p["w_b"])
    x1, idx, gate, rank, cnt = _outproj(x2d, m, p["w_o"], p["g_ffn"], p["w_r2"], p["b_r"])

    tm = MOE_ROWS
    counts = cnt[:, 0].astype(I32)
    padded = (counts + tm - 1) // tm * tm
    pad_end = jnp.cumsum(padded)
    pad_start = pad_end - padded
    experts = jnp.arange(N_EXPERTS, dtype=I32)
    dest = jnp.sum(jnp.where(idx[:, :, None] == experts, pad_start, 0), axis=-1) + rank
    nb = -(-(n * TOP_K) // tm) + N_EXPERTS
    rows = nb * tm
    blk_start = jnp.arange(nb, dtype=I32) * tm
    block_e = jnp.minimum(jnp.sum((pad_end[None, :] <= blk_start[:, None]).astype(I32), axis=1), N_EXPERTS - 1)
    n_used = (pad_end[-1:] // tm).astype(I32)

    xs = _dispatch(pad_start + counts, pad_end, n_used, dest, x1, p["g_ffn"], rows)
    ys = _experts(block_e, n_used, xs, p["w_gu_t"], p["b_g"], p["b_u"], p["w_d"], p["b_d"])
    out = _combine(dest, x1, gate, p["g_fin"], ys)
    return out.reshape(batch, seq, D_MODEL)


def kernel(x_prompt, x_sample, norm_mix_g, w_in, conv_w, conv_b, rg_w_a, rg_b_a, rg_w_x, rg_b_x, rg_lambda,
           w_proj_a, q_norm_g, w_uq, kv_norm_g, w_uk, w_uv, w_proj_b, w_out, norm_ffn_g, w_router, b_router,
           w_gate_up, b_gate_up, w_down, b_down, norm_final_g):
    p = _prep_weights(norm_mix_g, w_in, conv_w, conv_b, rg_w_a, rg_b_a, rg_w_x, rg_b_x, rg_lambda,
                      w_proj_a, q_norm_g, w_uq, kv_norm_g, w_uk, w_uv, w_proj_b, w_out,
                      norm_ffn_g, w_router, b_router, w_gate_up, b_gate_up, w_down, b_down, norm_final_g)
    return (_trunk(x_prompt, p), _trunk(x_sample, p))
```

```python
import functools
import math

import jax
import jax.numpy as jnp
from jax import lax
from jax.experimental import pallas as pl
from jax.experimental.pallas import tpu as pltpu

F32 = jnp.float32
BF16 = jnp.bfloat16
I32 = jnp.int32

D_MODEL = 2048
D_RNN = 1024
RNN_BLOCKS = 8
BLOCK_W = 128
C_RG = 8.0
N_HEADS = 16
QK_NOPE = 128
QK_ROPE = 64
QK_HEAD = QK_NOPE + QK_ROPE
V_HEAD = 128
Q_LORA = 512
KV_LORA = 512
ROPE_THETA = 10000.0
N_EXPERTS = 32
TOP_K = 4
D_FF = 2048
SWIGLU_LIMIT = 7.0
SWIGLU_ALPHA = 1.702
EPS = 1e-6

LANES = 128
VMEM_LIMIT = 56 * 1024 * 1024
NEG_BIG = -1e30

Z_GLA, Z_GLB, Z_XA, Z_GA, Z_CQ, Z_CKV = 0, 2048, 4096, 5120, 6144, 6656
Z_COLS = 7168


def _cparams(sem):
    return pltpu.CompilerParams(dimension_semantics=sem, vmem_limit_bytes=VMEM_LIMIT)


def _tile(n, pref):
    t = min(n, pref)
    assert n % t == 0, (n, pref)
    return t


def _rms(x, g):
    return x * lax.rsqrt(jnp.mean(x * x, axis=-1, keepdims=True) + EPS) * g


def _in_proj_body(x_ref, g_ref, w_ref, wk_ref, z_ref, zk_ref, u_ref):
    @pl.when(pl.program_id(1) == 0)
    def _():
        u = _rms(x_ref[...], g_ref[...]).astype(BF16)
        u_ref[...] = u
        zk_ref[...] = jnp.dot(u, wk_ref[...], preferred_element_type=F32).astype(BF16)

    z_ref[...] = jnp.dot(u_ref[...], w_ref[...], preferred_element_type=F32).astype(BF16)


def _in_proj(x2d, g, w_main, w_k2):
    n = x2d.shape[0]
    tm = _tile(n, 1024)
    tn = 1024
    return pl.pallas_call(
        _in_proj_body,
        grid=(n // tm, Z_COLS // tn),
        in_specs=[
            pl.BlockSpec((tm, D_MODEL), lambda i, j: (i, 0)),
            pl.BlockSpec((1, D_MODEL), lambda i, j: (0, 0)),
            pl.BlockSpec((D_MODEL, tn), lambda i, j: (0, j)),
            pl.BlockSpec((D_MODEL, LANES), lambda i, j: (0, 0)),
        ],
        out_specs=[
            pl.BlockSpec((tm, tn), lambda i, j: (i, j)),
            pl.BlockSpec((tm, LANES), lambda i, j: (i, 0)),
        ],
        out_shape=[
            jax.ShapeDtypeStruct((n, Z_COLS), BF16),
            jax.ShapeDtypeStruct((n, LANES), BF16),
        ],
        scratch_shapes=[pltpu.VMEM((tm, D_MODEL), BF16)],
        compiler_params=_cparams(("arbitrary", "arbitrary")),
        name="in_proj",
    )(x2d, g, w_main, w_k2)


HALO = 16
SCAN_ROWS = 128


def _sigmoid(x):
    return 1.0 / (1.0 + jnp.exp(-x))


def _gelu_tanh(x):
    return 0.5 * x * (1.0 + jnp.tanh(math.sqrt(2.0 / math.pi) * (x + 0.044715 * (x * x * x))))


def _scan_chunk(a, b, reverse):
    rows = a.shape[0]
    row = lax.broadcasted_iota(I32, a.shape, 0)
    d = 1
    while d < rows:
        if d < 8:
            shift = (rows - d) if reverse else d
            a_sh = pltpu.roll(a, shift, axis=0)
            b_sh = pltpu.roll(b, shift, axis=0)
            keep = (row < rows - d) if reverse else (row >= d)
            a_sh = jnp.where(keep, a_sh, 1.0)
            b_sh = jnp.where(keep, b_sh, 0.0)
        else:
            one = jnp.ones((d, a.shape[1]), F32)
            zero = jnp.zeros((d, a.shape[1]), F32)
            if reverse:
                a_sh = jnp.concatenate([a[d:], one], axis=0)
                b_sh = jnp.concatenate([b[d:], zero], axis=0)
            else:
                a_sh = jnp.concatenate([one, a[:-d]], axis=0)
                b_sh = jnp.concatenate([zero, b[:-d]], axis=0)
        b = a * b_sh + b
        a = a * a_sh
        d *= 2
    return a, b


def _rglru_body(xa_ref, ga_ref, cw_ref, cb_ref, w4_ref, b4_ref, lam_ref, o_ref,
                a0_ref, b0_ref, a1_ref, b1_ref, *, seq, tc):
    nchunk = seq // tc
    lam = lam_ref[0]
    sp = jnp.maximum(-lam, 0.0) + jnp.log1p(jnp.exp(-jnp.abs(lam)))
    sp0 = sp[:, :BLOCK_W]
    sp1 = sp[:, BLOCK_W:]
    cw = cw_ref[...]
    cb = cb_ref[...]
    w4 = w4_ref[0]
    b4 = b4_ref[0]

    def gates(j, carry):
        r0 = pl.multiple_of(j * tc, tc)
        cur = xa_ref[pl.ds(r0, tc), :].astype(F32)
        prev_start = pl.multiple_of(jnp.maximum(r0 - HALO, 0), HALO)
        next_start = pl.multiple_of(jnp.minimum(r0 + tc, seq - HALO), HALO)
        prev = xa_ref[pl.ds(prev_start, HALO), :].astype(F32)
        nxt = xa_ref[pl.ds(next_start, HALO), :].astype(F32)
        prev = jnp.where(j > 0, prev, 0.0)
        nxt = jnp.where(j < nchunk - 1, nxt, 0.0)
        ext = jnp.concatenate([prev, cur, nxt], axis=0)
        n_ext = tc + 2 * HALO
        xm2 = pltpu.roll(ext, 2, axis=0)[HALO:HALO + tc]
        xm1 = pltpu.roll(ext, 1, axis=0)[HALO:HALO + tc]
        xp1 = pltpu.roll(ext, n_ext - 1, axis=0)[HALO:HALO + tc]
        xc = xm2 * cw[0:1] + xm1 * cw[1:2] + cur * cw[2:3] + xp1 * cw[3:4] + cb
        pre = jnp.dot(xc.astype(BF16), w4, preferred_element_type=F32) + b4
        for d, (a_ref, b_ref, spd) in enumerate(((a0_ref, b0_ref, sp0), (a1_ref, b1_ref, sp1))):
            r = _sigmoid(pre[:, (2 * d) * BLOCK_W:(2 * d + 1) * BLOCK_W])
            i = _sigmoid(pre[:, (2 * d + 1) * BLOCK_W:(2 * d + 2) * BLOCK_W])
            a = jnp.exp((-C_RG) * r * spd)
            b = jnp.sqrt(1.0 - a * a) * (i * xc)
            a_ref[pl.ds(r0, tc), :] = a
            b_ref[pl.ds(r0, tc), :] = b
        return carry

    lax.fori_loop(0, nchunk, gates, 0)

    nscan = seq // SCAN_ROWS

    def bwd(jj, h):
        j = nscan - 1 - jj
        r0 = pl.multiple_of(j * SCAN_ROWS, SCAN_ROWS)
        A, B = _scan_chunk(a1_ref[pl.ds(r0, SCAN_ROWS), :], b1_ref[pl.ds(r0, SCAN_ROWS), :], True)
        hh = A * h + B
        b1_ref[pl.ds(r0, SCAN_ROWS), :] = hh
        return hh[0:1, :]

    lax.fori_loop(0, nscan, bwd, jnp.zeros((1, BLOCK_W), F32))

    def fwd(j, h):
        r0 = pl.multiple_of(j * SCAN_ROWS, SCAN_ROWS)
        A, B = _scan_chunk(a0_ref[pl.ds(r0, SCAN_ROWS), :], b0_ref[pl.ds(r0, SCAN_ROWS), :], False)
        hh = A * h + B
        ga = ga_ref[pl.ds(r0, SCAN_ROWS), :].astype(F32)
        o_ref[pl.ds(r0, SCAN_ROWS), :] = ((hh + b1_ref[pl.ds(r0, SCAN_ROWS), :]) * _gelu_tanh(ga)).astype(BF16)
        return hh[SCAN_ROWS - 1:SCAN_ROWS, :]

    lax.fori_loop(0, nscan, fwd, jnp.zeros((1, BLOCK_W), F32))


def _rglru(z, conv_w, conv_b, w4, b4, lam4, batch, seq):
    n = batch * seq
    tc = _tile(seq, 256)
    xa_blk = Z_XA // BLOCK_W
    ga_blk = Z_GA // BLOCK_W
    return pl.pallas_call(
        functools.partial(_rglru_body, seq=seq, tc=tc),
        grid=(batch, RNN_BLOCKS),
        in_specs=[
            pl.BlockSpec((seq, BLOCK_W), lambda b, c: (b, xa_blk + c)),
            pl.BlockSpec((seq, BLOCK_W), lambda b, c: (b, ga_blk + c)),
            pl.BlockSpec((4, BLOCK_W), lambda b, c: (0, c)),
            pl.BlockSpec((1, BLOCK_W), lambda b, c: (0, c)),
            pl.BlockSpec((1, BLOCK_W, 4 * BLOCK_W), lambda b, c: (c, 0, 0)),
            pl.BlockSpec((1, 1, 4 * BLOCK_W), lambda b, c: (c, 0, 0)),
            pl.BlockSpec((1, 1, 2 * BLOCK_W), lambda b, c: (c, 0, 0)),
        ],
        out_specs=pl.BlockSpec((seq, BLOCK_W), lambda b, c: (b, c)),
        out_shape=jax.ShapeDtypeStruct((n, D_RNN), BF16),
        scratch_shapes=[pltpu.VMEM((seq, BLOCK_W), F32) for _ in range(4)],
        compiler_params=_cparams(("arbitrary", "arbitrary")),
        name="rglru",
    )(z, z, conv_w, conv_b, w4, b4, lam4)


KV_CHUNK = 512


def _qkv_body(cq_ref, ckv_ref, zk_ref, ct_ref, st_ref, qg_ref, kvg_ref,
              wqn_ref, wqr_ref, wk_ref, wvt_ref,
              qn_ref, qr_ref, kn_ref, kr_ref, vt_ref):
    scale = math.log2(math.e) / math.sqrt(QK_HEAD)
    ct = ct_ref[...]
    st = st_ref[...]
    cqn = _rms(cq_ref[...].astype(F32), qg_ref[...]).astype(BF16)
    ckvn = _rms(ckv_ref[...].astype(F32), kvg_ref[...]).astype(BF16)
    qn = jnp.dot(cqn, wqn_ref[...], preferred_element_type=F32)
    qn_ref[...] = (qn * scale).astype(BF16)
    qp = jnp.dot(cqn, wqr_ref[...], preferred_element_type=F32)
    cts = ct * scale
    sts = st * scale
    for h in range(N_HEADS):
        p = qp[:, h * LANES:(h + 1) * LANES]
        qr_ref[:, h * LANES:(h + 1) * LANES] = (p * cts + pltpu.roll(p, LANES // 2, axis=1) * sts).astype(BF16)
    kn_ref[...] = jnp.dot(ckvn, wk_ref[...], preferred_element_type=F32).astype(BF16)
    zk = zk_ref[...].astype(F32)
    kr_ref[...] = (zk * ct + pltpu.roll(zk, LANES // 2, axis=1) * st).astype(BF16)
    vt = lax.dot_general(wvt_ref[...], ckvn, (((1,), (1,)), ((), ())), preferred_element_type=F32)
    vt_ref[0] = vt.astype(BF16)


def _qkv(z, zk, ctab, stab, qg, kvg, wqn, wqr, wk, wvt, seq):
    n = z.shape[0]
    t = _tile(seq, KV_CHUNK)
    spt = seq // t
    cq_blk = Z_CQ // Q_LORA
    ckv_blk = Z_CKV // KV_LORA
    hd = N_HEADS * LANES
    const = lambda i: (0, 0)
    return pl.pallas_call(
        _qkv_body,
        grid=(n // t,),
        in_specs=[
            pl.BlockSpec((t, Q_LORA), lambda i: (i, cq_blk)),
            pl.BlockSpec((t, KV_LORA), lambda i: (i, ckv_blk)),
            pl.BlockSpec((t, LANES), lambda i: (i, 0)),
            pl.BlockSpec((t, LANES), lambda i: (i % spt, 0)),
            pl.BlockSpec((t, LANES), lambda i: (i % spt, 0)),
            pl.BlockSpec((1, Q_LORA), const),
            pl.BlockSpec((1, KV_LORA), const),
            pl.BlockSpec((Q_LORA, hd), const),
            pl.BlockSpec((Q_LORA, hd), const),
            pl.BlockSpec((KV_LORA, hd), const),
            pl.BlockSpec((hd, KV_LORA), const),
        ],
        out_specs=[
            pl.BlockSpec((t, hd), lambda i: (i, 0)),
            pl.BlockSpec((t, hd), lambda i: (i, 0)),
            pl.BlockSpec((t, hd), lambda i: (i, 0)),
            pl.BlockSpec((t, LANES), lambda i: (i, 0)),
            pl.BlockSpec((1, hd, t), lambda i: (i, 0, 0)),
        ],
        out_shape=[
            jax.ShapeDtypeStruct((n, hd), BF16),
            jax.ShapeDtypeStruct((n, hd), BF16),
            jax.ShapeDtypeStruct((n, hd), BF16),
            jax.ShapeDtypeStruct((n, LANES), BF16),
            jax.ShapeDtypeStruct((n // t, hd, t), BF16),
        ],
        compiler_params=_cparams(("arbitrary",)),
        name="qkv",
    )(z, z, zk, ctab, stab, qg, kvg, wqn, wqr, wk, wvt)


ATTN_CHAINS = 2
ONES_PAD = 16


def _attn_body(qn_ref, qr_ref, kn_ref, kr_ref, vt_ref, o_ref, *, nk, kc):
    tq = qn_ref.shape[0]
    th = tq // ATTN_CHAINS
    qs = [jnp.concatenate([qn_ref[h * th:(h + 1) * th, :], qr_ref[h * th:(h + 1) * th, :]], axis=1)
          for h in range(ATTN_CHAINS)]

    def keys(c):
        return jnp.concatenate([kn_ref[c * kc:(c + 1) * kc, :], kr_ref[c * kc:(c + 1) * kc, :]], axis=1)

    def scores(k, h):
        return lax.dot_general(k, qs[h], (((1,), (1,)), ((), ())), preferred_element_type=F32)

    ones_rows = jnp.where(lax.broadcasted_iota(I32, (ONES_PAD, kc), 0) == 0, 1.0, 0.0).astype(BF16)
    m = [jnp.full((1, th), NEG_BIG, F32) for _ in range(ATTN_CHAINS)]
    acc = [jnp.zeros((V_HEAD + ONES_PAD, th), F32) for _ in range(ATTN_CHAINS)]
    k_first = keys(0)
    s_next = [scores(k_first, h) for h in range(ATTN_CHAINS)]
    for c in range(nk):
        s_cur = s_next
        if c + 1 < nk:
            k_ahead = keys(c + 1)
            s_next = [scores(k_ahead, h) for h in range(ATTN_CHAINS)]
        v_ext = jnp.concatenate([vt_ref[c], ones_rows], axis=0)
        for h in range(ATTN_CHAINS):
            s = s_cur[h]
            m_new = jnp.maximum(m[h], jnp.max(s, axis=0, keepdims=True))
            alpha = jnp.exp2(m[h] - m_new)
            p = jnp.exp2((s - m_new).astype(BF16))
            acc[h] = alpha * acc[h] + jnp.dot(v_ext, p, preferred_element_type=F32)
            m[h] = m_new
    for h in range(ATTN_CHAINS):
        o = acc[h][:V_HEAD] * (1.0 / acc[h][V_HEAD:V_HEAD + 1])
        o_ref[h * th:(h + 1) * th, :] = o.T.astype(BF16)


def _attention(qn, qr, kn, kr, vt, batch, seq):
    n = batch * seq
    tq = _tile(seq, 1024)
    kc = _tile(seq, KV_CHUNK)
    nq = seq // tq
    nk = seq // kc
    return pl.pallas_call(
        functools.partial(_attn_body, nk=nk, kc=kc),
        grid=(batch, N_HEADS, nq),
        in_specs=[
            pl.BlockSpec((tq, LANES), lambda b, h, i: (b * nq + i, h)),
            pl.BlockSpec((tq, LANES), lambda b, h, i: (b * nq + i, h)),
            pl.BlockSpec((seq, LANES), lambda b, h, i: (b, h)),
            pl.BlockSpec((seq, LANES), lambda b, h, i: (b, 0)),
            pl.BlockSpec((nk, V_HEAD, kc), lambda b, h, i: (b, h, 0)),
        ],
        out_specs=pl.BlockSpec((tq, V_HEAD), lambda b, h, i: (b * nq + i, h)),
        out_shape=jax.ShapeDtypeStruct((n, N_HEADS * V_HEAD), BF16),
        compiler_params=_cparams(("arbitrary", "arbitrary", "arbitrary")),
        name="attn",
    )(qn, qr, kn, kr, vt)


def _merge_body(hg_ref, at_ref, gla_ref, glb_ref, wa_ref, wb_ref, m_ref):
    ya = jnp.dot(hg_ref[...], wa_ref[...], preferred_element_type=F32)
    yb = jnp.dot(at_ref[...], wb_ref[...], preferred_element_type=F32)
    m = _sigmoid(gla_ref[...].astype(F32)) * ya + _sigmoid(glb_ref[...].astype(F32)) * yb
    m_ref[...] = m.astype(BF16)


def _merge(hg, attn, z, wa, wb):
    n = hg.shape[0]
    t = _tile(n, 256)
    const = lambda i: (0, 0)
    return pl.pallas_call(
        _merge_body,
        grid=(n // t,),
        in_specs=[
            pl.BlockSpec((t, D_RNN), lambda i: (i, 0)),
            pl.BlockSpec((t, D_MODEL), lambda i: (i, 0)),
            pl.BlockSpec((t, D_MODEL), lambda i: (i, Z_GLA // D_MODEL)),
            pl.BlockSpec((t, D_MODEL), lambda i: (i, Z_GLB // D_MODEL)),
            pl.BlockSpec((D_RNN, D_MODEL), const),
            pl.BlockSpec((D_MODEL, D_MODEL), const),
        ],
        out_specs=pl.BlockSpec((t, D_MODEL), lambda i: (i, 0)),
        out_shape=jax.ShapeDtypeStruct((n, D_MODEL), BF16),
        compiler_params=_cparams(("arbitrary",)),
        name="merge",
    )(hg, attn, z, z, wa, wb)


def _outproj_body(x_ref, m_ref, wo_ref, g_ref, wr_ref, br_ref,
                  x1_ref, idx_ref, gate_ref, rank_ref, cnt_ref, carry_ref):
    t = x_ref.shape[0]

    @pl.when(pl.program_id(0) == 0)
    def _():
        carry_ref[...] = jnp.zeros(carry_ref.shape, F32)

    x1 = x_ref[...] + jnp.dot(m_ref[...], wo_ref[...], preferred_element_type=F32)
    x1_ref[...] = x1
    un = _rms(x1, g_ref[...])
    un_hi = un.astype(BF16)
    un_lo = (un - un_hi.astype(F32)).astype(BF16)
    w_hi = wr_ref[0]
    logits = (jnp.dot(un_hi, w_hi, preferred_element_type=F32)
              + jnp.dot(un_lo, w_hi, preferred_element_type=F32)
              + jnp.dot(un_hi, wr_ref[1], preferred_element_type=F32)).T + br_ref[...]
    e_iota = lax.broadcasted_iota(I32, logits.shape, 0)
    vals, idxs, sels = [], [], []
    cur = logits
    for _ in range(TOP_K):
        mx = jnp.max(cur, axis=0, keepdims=True)
        idx = jnp.min(jnp.where(cur == mx, e_iota, N_EXPERTS), axis=0, keepdims=True)
        sel = e_iota == idx
        vals.append(mx)
        idxs.append(idx)
        sels.append(sel)
        cur = jnp.where(sel, -jnp.inf, cur)
    ex = [jnp.exp(v - vals[0]) for v in vals]
    inv = 1.0 / (ex[0] + ex[1] + ex[2] + ex[3])
    onehot = jnp.where(sels[0] | sels[1] | sels[2] | sels[3], 1.0, 0.0)
    row = lax.broadcasted_iota(I32, (t, t), 0)
    col = lax.broadcasted_iota(I32, (t, t), 1)
    upper = jnp.where(row < col, 1.0, 0.0).astype(BF16)
    prefix = jnp.dot(onehot.astype(BF16), upper, preferred_element_type=F32) + carry_ref[:, 0:1]
    for k in range(TOP_K):
        idx_ref[k:k + 1, :] = idxs[k]
        gate_ref[k:k + 1, :] = ex[k] * inv
        rank_ref[k:k + 1, :] = jnp.sum(jnp.where(sels[k], prefix, 0.0), axis=0, keepdims=True).astype(I32)
    gate_ref[TOP_K:, :] = jnp.zeros((gate_ref.shape[0] - TOP_K, t), F32)
    carry_ref[...] = carry_ref[...] + jnp.sum(onehot, axis=1, keepdims=True)
    cnt_ref[...] = carry_ref[...]


def _outproj(x2d, m, wo, g, wr2, br):
    n = x2d.shape[0]
    t = _tile(n, 256)
    const = lambda i: (0, 0)
    return pl.pallas_call(
        _outproj_body,
        grid=(n // t,),
        in_specs=[
            pl.BlockSpec((t, D_MODEL), lambda i: (i, 0)),
            pl.BlockSpec((t, D_MODEL), lambda i: (i, 0)),
            pl.BlockSpec((D_MODEL, D_MODEL), const),
            pl.BlockSpec((1, D_MODEL), const),
            pl.BlockSpec((2, D_MODEL, N_EXPERTS), lambda i: (0, 0, 0)),
            pl.BlockSpec((N_EXPERTS, 1), const),
        ],
        out_specs=[
            pl.BlockSpec((t, D_MODEL), lambda i: (i, 0)),
            pl.BlockSpec((TOP_K, t), lambda i: (0, i)),
            pl.BlockSpec((8, t), lambda i: (0, i)),
            pl.BlockSpec((TOP_K, t), lambda i: (0, i)),
            pl.BlockSpec((N_EXPERTS, LANES), const),
        ],
        out_shape=[
            jax.ShapeDtypeStruct((n, D_MODEL), F32),
            jax.ShapeDtypeStruct((TOP_K, n), I32),
            jax.ShapeDtypeStruct((8, n), F32),
            jax.ShapeDtypeStruct((TOP_K, n), I32),
            jax.ShapeDtypeStruct((N_EXPERTS, LANES), F32),
        ],
        scratch_shapes=[pltpu.VMEM((N_EXPERTS, LANES), F32)],
        compiler_params=_cparams(("arbitrary",)),
        name="outproj",
    )(x2d, m, wo, g, wr2, br)


ZERO_ROWS = 256


def _dispatch_body(fs_ref, fe_ref, nu_ref, dest_ref, x1_ref, g_ref, xs_ref, un_ref, zero_ref, sem, zsem,
                   *, nsteps, nblocks):
    t = x1_ref.shape[0]
    i = pl.program_id(0)
    slot = i % 2

    def wait_slot(s):
        for _ in range(TOP_K):
            pltpu.make_async_copy(un_ref.at[s], xs_ref.at[pl.ds(0, t), :], sem.at[s]).wait()

    @pl.when(i >= 2)
    def _():
        wait_slot(slot)

    un_ref[slot] = _rms(x1_ref[...], g_ref[...])

    def issue(r, carry):
        for k in range(TOP_K):
            pltpu.make_async_copy(un_ref.at[slot, pl.ds(r, 1), :],
                                  xs_ref.at[pl.ds(dest_ref[k, r], 1), :], sem.at[slot]).start()
        return carry

    lax.fori_loop(0, t, issue, 0, unroll=8)

    @pl.when(i == 0)
    def _():
        zero_ref[...] = jnp.zeros(zero_ref.shape, F32)

        def fill_expert(e, carry):
            def start(r, c):
                pltpu.make_async_copy(zero_ref.at[pl.ds(0, 1), :], xs_ref.at[pl.ds(r, 1), :], zsem).start()
                return c

            def wait(r, c):
                pltpu.make_async_copy(zero_ref.at[pl.ds(0, 1), :], xs_ref.at[pl.ds(0, 1), :], zsem).wait()
                return c

            lax.fori_loop(fs_ref[e], fe_ref[e], start, 0)
            lax.fori_loop(fs_ref[e], fe_ref[e], wait, 0)
            return carry

        lax.fori_loop(0, N_EXPERTS, fill_expert, 0)

        def fill_block(b, carry):
            for part in range(MOE_ROWS // ZERO_ROWS):
                r0 = pl.multiple_of(b * MOE_ROWS + part * ZERO_ROWS, ZERO_ROWS)
                cp = pltpu.make_async_copy(zero_ref, xs_ref.at[pl.ds(r0, ZERO_ROWS), :], zsem)
                cp.start()
                cp.wait()
            return carry

        lax.fori_loop(nu_ref[0], nblocks, fill_block, 0)

    @pl.when(i == nsteps - 1)
    def _():
        wait_slot(slot)
        if nsteps >= 2:
            wait_slot(1 - slot)


def _dispatch(fill_start, fill_end, n_used, dest, x1, g, rows):
    n = x1.shape[0]
    t = _tile(n, 256)
    nsteps = n // t
    grid_spec = pltpu.PrefetchScalarGridSpec(
        num_scalar_prefetch=3,
        grid=(nsteps,),
        in_specs=[
            pl.BlockSpec((TOP_K, t), lambda i, fs, fe, nu: (0, i), memory_space=pltpu.SMEM),
            pl.BlockSpec((t, D_MODEL), lambda i, fs, fe, nu: (i, 0)),
            pl.BlockSpec((1, D_MODEL), lambda i, fs, fe, nu: (0, 0)),
        ],
        out_specs=pl.BlockSpec(memory_space=pl.ANY),
        scratch_shapes=[pltpu.VMEM((2, t, D_MODEL), F32), pltpu.VMEM((ZERO_ROWS, D_MODEL), F32),
                        pltpu.SemaphoreType.DMA((2,)), pltpu.SemaphoreType.DMA(())],
    )
    return pl.pallas_call(
        functools.partial(_dispatch_body, nsteps=nsteps, nblocks=rows // MOE_ROWS),
        grid_spec=grid_spec,
        out_shape=jax.ShapeDtypeStruct((rows, D_MODEL), F32),
        compiler_params=_cparams(("arbitrary",)),
        name="dispatch",
    )(fill_start, fill_end, n_used, dest, x1, g)


MOE_ROWS = 512
MOE_FF = 1024


def _experts_body(be_ref, nu_ref, xs_ref, wg_ref, wu_ref, bg_ref, bu_ref, wd_ref, bd_ref,
                  ys_ref, xb_ref):
    i = pl.program_id(0)
    f = pl.program_id(1)

    @pl.when(jnp.logical_and(i >= nu_ref[0], f == 0))
    def _():
        ys_ref[...] = jnp.zeros(ys_ref.shape, F32)

    @pl.when(i < nu_ref[0])
    def _():
        @pl.when(f == 0)
        def _():
            xb_ref[...] = xs_ref[...].astype(BF16)

        xb = xb_ref[...]
        nt = (((1,), (1,)), ((), ()))
        g = lax.dot_general(xb, wg_ref[0], nt, preferred_element_type=F32) + bg_ref[0]
        u = lax.dot_general(xb, wu_ref[0], nt, preferred_element_type=F32) + bu_ref[0]
        g = jnp.minimum(g, SWIGLU_LIMIT)
        u = jnp.clip(u, -SWIGLU_LIMIT, SWIGLU_LIMIT)
        h = (g * _sigmoid(SWIGLU_ALPHA * g) * (u + 1.0)).astype(BF16)
        y = jnp.dot(h, wd_ref[0], preferred_element_type=F32)

        @pl.when(f == 0)
        def _():
            ys_ref[...] = y + bd_ref[0]

        @pl.when(f > 0)
        def _():
            ys_ref[...] = ys_ref[...] + y


def _experts(block_e, n_used, xs, wg_t, wu_t, bg, bu, wd, bd):
    rows = xs.shape[0]
    tm = MOE_ROWS
    nb = rows // tm
    nf = D_FF // MOE_FF

    def blk(i, nu):
        return jnp.minimum(i, nu[0] - 1)

    def fidx(i, f, nu):
        return jnp.where(i < nu[0], f, nf - 1)

    grid_spec = pltpu.PrefetchScalarGridSpec(
        num_scalar_prefetch=2,
        grid=(nb, nf),
        in_specs=[
            pl.BlockSpec((tm, D_MODEL), lambda i, f, be, nu: (blk(i, nu), 0)),
            pl.BlockSpec((1, MOE_FF, D_MODEL), lambda i, f, be, nu: (be[blk(i, nu)], fidx(i, f, nu), 0)),
            pl.BlockSpec((1, MOE_FF, D_MODEL), lambda i, f, be, nu: (be[blk(i, nu)], fidx(i, f, nu), 0)),
            pl.BlockSpec((1, 1, MOE_FF), lambda i, f, be, nu: (be[blk(i, nu)], 0, fidx(i, f, nu))),
            pl.BlockSpec((1, 1, MOE_FF), lambda i, f, be, nu: (be[blk(i, nu)], 0, fidx(i, f, nu))),
            pl.BlockSpec((1, MOE_FF, D_MODEL), lambda i, f, be, nu: (be[blk(i, nu)], fidx(i, f, nu), 0)),
            pl.BlockSpec((1, 1, D_MODEL), lambda i, f, be, nu: (be[blk(i, nu)], 0, 0)),
        ],
        out_specs=pl.BlockSpec((tm, D_MODEL), lambda i, f, be, nu: (i, 0)),
        scratch_shapes=[pltpu.VMEM((tm, D_MODEL), BF16)],
    )
    return pl.pallas_call(
        _experts_body,
        grid_spec=grid_spec,
        out_shape=jax.ShapeDtypeStruct((rows, D_MODEL), F32),
        compiler_params=_cparams(("arbitrary", "arbitrary")),
        name="experts",
    )(block_e, n_used, xs, wg_t, wu_t, bg, bu, wd, bd)


def _combine_body(dest_ref, x1_ref, gate_ref, g_ref, ys_ref, o_ref, buf_ref, sem):
    t = x1_ref.shape[0]

    def issue(r, carry):
        for k in range(TOP_K):
            pltpu.make_async_copy(ys_ref.at[pl.ds(dest_ref[k, r], 1), :],
                                  buf_ref.at[k, pl.ds(r, 1), :], sem).start()
        return carry

    lax.fori_loop(0, t, issue, 0, unroll=8)
    for k in range(TOP_K):
        pltpu.make_async_copy(ys_ref.at[pl.ds(0, t), :], buf_ref.at[k], sem).wait()
    gcol = gate_ref[...].T
    y = x1_ref[...]
    for k in range(TOP_K):
        y = y + gcol[:, k:k + 1] * buf_ref[k]
    o_ref[...] = _rms(y, g_ref[...])


def _combine(dest, x1, gate, g, ys):
    n = x1.shape[0]
    t = _tile(n, 256)
    return pl.pallas_call(
        _combine_body,
        grid=(n // t,),
        in_specs=[
            pl.BlockSpec((TOP_K, t), lambda i: (0, i), memory_space=pltpu.SMEM),
            pl.BlockSpec((t, D_MODEL), lambda i: (i, 0)),
            pl.BlockSpec((8, t), lambda i: (0, i)),
            pl.BlockSpec((1, D_MODEL), lambda i: (0, 0)),
            pl.BlockSpec(memory_space=pl.ANY),
        ],
        out_specs=pl.BlockSpec((t, D_MODEL), lambda i: (i, 0)),
        out_shape=jax.ShapeDtypeStruct((n, D_MODEL), F32),
        scratch_shapes=[pltpu.VMEM((TOP_K, t, D_MODEL), F32), pltpu.SemaphoreType.DMA(())],
        compiler_params=_cparams(("arbitrary",)),
        name="combine",
    )(dest, x1, gate, g, ys)


PREP_K = 512


def _prep_gate_up_body(w_ref, g_ref, u_ref, t_ref):
    wt = w_ref[0].T
    for c in range(PREP_K // LANES):
        cols = slice(c * LANES, (c + 1) * LANES)
        t_ref[c] = wt[:, cols]
        g_ref[0, :, cols] = t_ref[c, pl.ds(0, D_FF, stride=2), :].astype(BF16)
        u_ref[0, :, cols] = t_ref[c, pl.ds(1, D_FF, stride=2), :].astype(BF16)


def _prep_gate_up(w_gate_up):
    out = jax.ShapeDtypeStruct((N_EXPERTS, D_FF, D_MODEL), BF16)
    return pl.pallas_call(
        _prep_gate_up_body,
        grid=(N_EXPERTS, D_MODEL // PREP_K),
        in_specs=[pl.BlockSpec((1, PREP_K, 2 * D_FF), lambda e, k: (e, k, 0))],
        out_specs=[pl.BlockSpec((1, D_FF, PREP_K), lambda e, k: (e, 0, k)),
                   pl.BlockSpec((1, D_FF, PREP_K), lambda e, k: (e, 0, k))],
        out_shape=[out, out],
        scratch_shapes=[pltpu.VMEM((PREP_K // LANES, 2 * D_FF, LANES), F32)],
        compiler_params=_cparams(("arbitrary", "arbitrary")),
        name="prep_gate_up",
    )(w_gate_up)


def _cast_body(w_ref, o_ref):
    o_ref[...] = w_ref[...].astype(BF16)


def _prep_cast(w):
    e, r, c = w.shape
    tr = _tile(r, 1024)
    return pl.pallas_call(
        _cast_body,
        grid=(e, r // tr),
        in_specs=[pl.BlockSpec((1, tr, c), lambda i, j: (i, j, 0))],
        out_specs=pl.BlockSpec((1, tr, c), lambda i, j: (i, j, 0)),
        out_shape=jax.ShapeDtypeStruct(w.shape, BF16),
        compiler_params=_cparams(("arbitrary", "arbitrary")),
        name="prep_cast",
    )(w)


def _rot_cols(w):
    half = QK_ROPE // 2
    return jnp.concatenate([-w[..., half:], w[..., :half]], axis=-1)


def _prep_weights(norm_mix_g, w_in, conv_w, conv_b, rg_w_a, rg_b_a, rg_w_x, rg_b_x, rg_lambda,
                  w_proj_a, q_norm_g, w_uq, kv_norm_g, w_uk, w_uv, w_proj_b, w_out,
                  norm_ffn_g, w_router, b_router, w_gate_up, b_gate_up, w_down, b_down, norm_final_g):
    p = {}
    wi = w_in[0]
    o_xa, o_ga, o_cq, o_ckv, o_kr, o_gla, o_glb = 0, 1024, 2048, 2560, 3072, 3136, 5184
    p["w_main"] = jnp.concatenate([
        wi[:, o_gla:o_gla + D_MODEL], wi[:, o_glb:o_glb + D_MODEL],
        wi[:, o_xa:o_xa + D_RNN], wi[:, o_ga:o_ga + D_RNN],
        wi[:, o_cq:o_cq + Q_LORA], wi[:, o_ckv:o_ckv + KV_LORA]], axis=1).astype(BF16)
    wkr = wi[:, o_kr:o_kr + QK_ROPE]
    p["w_k2"] = jnp.concatenate([wkr, _rot_cols(wkr)], axis=1).astype(BF16)
    p["g_mix"] = norm_mix_g[0].reshape(1, D_MODEL)
    p["conv_w"] = conv_w[0]
    p["conv_b"] = conv_b[0].reshape(1, D_RNN)
    wa, wx = rg_w_a[0], rg_w_x[0]
    p["w4"] = jnp.concatenate([wa[0], wx[0], wa[1], wx[1]], axis=-1).astype(BF16)
    ba = rg_b_a[0].reshape(2, RNN_BLOCKS, 1, BLOCK_W)
    bx = rg_b_x[0].reshape(2, RNN_BLOCKS, 1, BLOCK_W)
    p["b4"] = jnp.concatenate([ba[0], bx[0], ba[1], bx[1]], axis=-1)
    lam = rg_lambda[0].reshape(2, RNN_BLOCKS, 1, BLOCK_W)
    p["lam4"] = jnp.concatenate([lam[0], lam[1]], axis=-1)
    p["w_a"] = w_proj_a[0].astype(BF16)
    p["q_g"] = q_norm_g[0].reshape(1, Q_LORA)
    p["kv_g"] = kv_norm_g[0].reshape(1, KV_LORA)
    wq = w_uq[0].reshape(Q_LORA, N_HEADS, QK_HEAD)
    p["wq_n"] = wq[:, :, :QK_NOPE].reshape(Q_LORA, N_HEADS * QK_NOPE).astype(BF16)
    wqr = wq[:, :, QK_NOPE:]
    p["wq_r"] = jnp.concatenate([wqr, _rot_cols(wqr)], axis=-1).reshape(Q_LORA, N_HEADS * LANES).astype(BF16)
    p["w_k"] = w_uk[0].astype(BF16)
    p["w_vt"] = w_uv[0].T.astype(BF16)
    p["w_b"] = w_proj_b[0].astype(BF16)
    p["w_o"] = w_out[0].astype(BF16)
    p["g_ffn"] = norm_ffn_g[0].reshape(1, D_MODEL)
    w_r_hi = w_router[0].astype(BF16)
    w_r_lo = (w_router[0] - w_r_hi.astype(F32)).astype(BF16)
    p["w_r2"] = jnp.stack([w_r_hi, w_r_lo])
    p["b_r"] = b_router[0].reshape(N_EXPERTS, 1)
    p["w_g_t"], p["w_u_t"] = _prep_gate_up(w_gate_up[0])
    bgu = b_gate_up[0]
    p["b_g"] = bgu[:, 0::2].reshape(N_EXPERTS, 1, D_FF)
    p["b_u"] = bgu[:, 1::2].reshape(N_EXPERTS, 1, D_FF)
    p["w_d"] = _prep_cast(w_down[0])
    p["b_d"] = b_down[0].reshape(N_EXPERTS, 1, D_MODEL)
    p["g_fin"] = norm_final_g.reshape(1, D_MODEL)
    return p


def _rope_tables(seq):
    half = QK_ROPE // 2
    freqs = ROPE_THETA ** (-jnp.arange(half, dtype=F32) / half)
    ang = jnp.arange(seq, dtype=F32)[:, None] * freqs[None, :]
    zero = jnp.zeros((seq, LANES - QK_ROPE), F32)
    ctab = jnp.concatenate([jnp.cos(ang), jnp.cos(ang), zero], axis=1)
    stab = jnp.concatenate([jnp.sin(ang), jnp.sin(ang), zero], axis=1)
    return ctab, stab


def _trunk(x, p):
    batch, seq, _ = x.shape
    n = batch * seq
    x2d = x.reshape(n, D_MODEL)
    z, zk = _in_proj(x2d, p["g_mix"], p["w_main"], p["w_k2"])
    hg = _rglru(z, p["conv_w"], p["conv_b"], p["w4"], p["b4"], p["lam4"], batch, seq)
    ctab, stab = _rope_tables(seq)
    qn, qr, kn, kr, vt = _qkv(z, zk, ctab, stab, p["q_g"], p["kv_g"],
                              p["wq_n"], p["wq_r"], p["w_k"], p["w_vt"], seq)
    attn = _attention(qn, qr, kn, kr, vt, batch, seq)
    m = _merge(hg, attn, z, p["w_a"], p["w_b"])
    x1, idx, gate, rank, cnt = _outproj(x2d, m, p["w_o"], p["g_ffn"], p["w_r2"], p["b_r"])

    tm = MOE_ROWS
    counts = cnt[:, 0].astype(I32)
    padded = (counts + tm - 1) // tm * tm
    pad_end = jnp.cumsum(padded)
    pad_start = pad_end - padded
    experts = jnp.arange(N_EXPERTS, dtype=I32)
    dest = jnp.sum(jnp.where(idx[:, :, None] == experts, pad_start, 0), axis=-1) + rank
    nb = -(-(n * TOP_K) // tm) + N_EXPERTS
    rows = nb * tm
    blk_start = jnp.arange(nb, dtype=I32) * tm
    block_e = jnp.minimum(jnp.sum((pad_end[None, :] <= blk_start[:, None]).astype(I32), axis=1), N_EXPERTS - 1)
    n_used = (pad_end[-1:] // tm).astype(I32)

    xs = _dispatch(pad_start + counts, pad_end, n_used, dest, x1, p["g_ffn"], rows)
    ys = _experts(block_e, n_used, xs, p["w_g_t"], p["w_u_t"], p["b_g"], p["b_u"], p["w_d"], p["b_d"])
    out = _combine(dest, x1, gate, p["g_fin"], ys)
    return out.reshape(batch, seq, D_MODEL)


def kernel(x_prompt, x_sample, norm_mix_g, w_in, conv_w, conv_b, rg_w_a, rg_b_a, rg_w_x, rg_b_x, rg_lambda,
           w_proj_a, q_norm_g, w_uq, kv_norm_g, w_uk, w_uv, w_proj_b, w_out, norm_ffn_g, w_router, b_router,
           w_gate_up, b_gate_up, w_down, b_down, norm_final_g):
    p = _prep_weights(norm_mix_g, w_in, conv_w, conv_b, rg_w_a, rg_b_a, rg_w_x, rg_b_x, rg_lambda,
                      w_proj_a, q_norm_g, w_uq, kv_norm_g, w_uk, w_uv, w_proj_b, w_out,
                      norm_ffn_g, w_router, b_router, w_gate_up, b_gate_up, w_down, b_down, norm_final_g)
    return (_trunk(x_prompt, p), _trunk(x_sample, p))
```

```python
import functools
import math

import jax
import jax.numpy as jnp
from jax import lax
from jax.experimental import pallas as pl
from jax.experimental.pallas import tpu as pltpu

F32 = jnp.float32
BF16 = jnp.bfloat16
I32 = jnp.int32

D_MODEL = 2048
D_RNN = 1024
RNN_BLOCKS = 8
BLOCK_W = 128
C_RG = 8.0
N_HEADS = 16
QK_NOPE = 128
QK_ROPE = 64
QK_HEAD = QK_NOPE + QK_ROPE
V_HEAD = 128
Q_LORA = 512
KV_LORA = 512
ROPE_THETA = 10000.0
N_EXPERTS = 32
TOP_K = 4
D_FF = 2048
SWIGLU_LIMIT = 7.0
SWIGLU_ALPHA = 1.702
EPS = 1e-6

LANES = 128
VMEM_LIMIT = 56 * 1024 * 1024
NEG_BIG = -1e30

Z_GLA, Z_GLB, Z_XA, Z_GA, Z_CQ, Z_CKV = 0, 2048, 4096, 5120, 6144, 6656
Z_COLS = 7168


def _cparams(sem):
    return pltpu.CompilerParams(dimension_semantics=sem, vmem_limit_bytes=VMEM_LIMIT)


def _tile(n, pref):
    t = min(n, pref)
    assert n % t == 0, (n, pref)
    return t


def _rms(x, g):
    return x * lax.rsqrt(jnp.mean(x * x, axis=-1, keepdims=True) + EPS) * g


def _in_proj_body(x_ref, g_ref, w_ref, wk_ref, z_ref, zk_ref, u_ref):
    @pl.when(pl.program_id(1) == 0)
    def _():
        u = _rms(x_ref[...], g_ref[...]).astype(BF16)
        u_ref[...] = u
        zk_ref[...] = jnp.dot(u, wk_ref[...], preferred_element_type=F32).astype(BF16)

    z_ref[...] = jnp.dot(u_ref[...], w_ref[...], preferred_element_type=F32).astype(BF16)


def _in_proj(x2d, g, w_main, w_k2):
    n = x2d.shape[0]
    tm = _tile(n, 1024)
    tn = 1024
    return pl.pallas_call(
        _in_proj_body,
        grid=(n // tm, Z_COLS // tn),
        in_specs=[
            pl.BlockSpec((tm, D_MODEL), lambda i, j: (i, 0)),
            pl.BlockSpec((1, D_MODEL), lambda i, j: (0, 0)),
            pl.BlockSpec((D_MODEL, tn), lambda i, j: (0, j)),
            pl.BlockSpec((D_MODEL, LANES), lambda i, j: (0, 0)),
        ],
        out_specs=[
            pl.BlockSpec((tm, tn), lambda i, j: (i, j)),
            pl.BlockSpec((tm, LANES), lambda i, j: (i, 0)),
        ],
        out_shape=[
            jax.ShapeDtypeStruct((n, Z_COLS), BF16),
            jax.ShapeDtypeStruct((n, LANES), BF16),
        ],
        scratch_shapes=[pltpu.VMEM((tm, D_MODEL), BF16)],
        compiler_params=_cparams(("arbitrary", "arbitrary")),
        name="in_proj",
    )(x2d, g, w_main, w_k2)


HALO = 16
SCAN_ROWS = 128


def _sigmoid(x):
    return 0.5 * jnp.tanh(0.5 * x) + 0.5


def _gelu_tanh(x):
    return 0.5 * x * (1.0 + jnp.tanh(math.sqrt(2.0 / math.pi) * (x + 0.044715 * (x * x * x))))


def _scan_chunk(a, b, reverse):
    rows = a.shape[0]
    row = lax.broadcasted_iota(I32, a.shape, 0)
    d = 1
    while d < rows:
        if d < 8:
            shift = (rows - d) if reverse else d
            a_sh = pltpu.roll(a, shift, axis=0)
            b_sh = pltpu.roll(b, shift, axis=0)
            keep = (row < rows - d) if reverse else (row >= d)
            a_sh = jnp.where(keep, a_sh, 1.0)
            b_sh = jnp.where(keep, b_sh, 0.0)
        else:
            one = jnp.ones((d, a.shape[1]), F32)
            zero = jnp.zeros((d, a.shape[1]), F32)
            if reverse:
                a_sh = jnp.concatenate([a[d:], one], axis=0)
                b_sh = jnp.concatenate([b[d:], zero], axis=0)
            else:
                a_sh = jnp.concatenate([one, a[:-d]], axis=0)
                b_sh = jnp.concatenate([zero, b[:-d]], axis=0)
        b = a * b_sh + b
        a = a * a_sh
        d *= 2
    return a, b


def _rglru_body(xa_ref, ga_ref, cw_ref, cb_ref, w4_ref, b4_ref, lam_ref, o_ref,
                a0_ref, b0_ref, a1_ref, b1_ref, *, seq, tc):
    nchunk = seq // tc
    lam = lam_ref[0]
    sp = jnp.maximum(-lam, 0.0) + jnp.log1p(jnp.exp(-jnp.abs(lam)))
    sp0 = sp[:, :BLOCK_W]
    sp1 = sp[:, BLOCK_W:]
    cw = cw_ref[...]
    cb = cb_ref[...]
    w4 = w4_ref[0]
    b4 = b4_ref[0]

    def gates(j, carry):
        r0 = pl.multiple_of(j * tc, tc)
        cur = xa_ref[pl.ds(r0, tc), :].astype(F32)
        prev_start = pl.multiple_of(jnp.maximum(r0 - HALO, 0), HALO)
        next_start = pl.multiple_of(jnp.minimum(r0 + tc, seq - HALO), HALO)
        prev = xa_ref[pl.ds(prev_start, HALO), :].astype(F32)
        nxt = xa_ref[pl.ds(next_start, HALO), :].astype(F32)
        prev = jnp.where(j > 0, prev, 0.0)
        nxt = jnp.where(j < nchunk - 1, nxt, 0.0)
        ext = jnp.concatenate([prev, cur, nxt], axis=0)
        n_ext = tc + 2 * HALO
        xm2 = pltpu.roll(ext, 2, axis=0)[HALO:HALO + tc]
        xm1 = pltpu.roll(ext, 1, axis=0)[HALO:HALO + tc]
        xp1 = pltpu.roll(ext, n_ext - 1, axis=0)[HALO:HALO + tc]
        xc = xm2 * cw[0:1] + xm1 * cw[1:2] + cur * cw[2:3] + xp1 * cw[3:4] + cb
        pre = jnp.dot(xc.astype(BF16), w4, preferred_element_type=F32) + b4
        for d, (a_ref, b_ref, spd) in enumerate(((a0_ref, b0_ref, sp0), (a1_ref, b1_ref, sp1))):
            r = _sigmoid(pre[:, (2 * d) * BLOCK_W:(2 * d + 1) * BLOCK_W])
            i = _sigmoid(pre[:, (2 * d + 1) * BLOCK_W:(2 * d + 2) * BLOCK_W])
            a = jnp.exp((-C_RG) * r * spd)
            b = jnp.sqrt(1.0 - a * a) * (i * xc)
            a_ref[pl.ds(r0, tc), :] = a
            b_ref[pl.ds(r0, tc), :] = b
        return carry

    lax.fori_loop(0, nchunk, gates, 0, unroll=2)

    nscan = seq // SCAN_ROWS

    def bwd(jj, h):
        j = nscan - 1 - jj
        r0 = pl.multiple_of(j * SCAN_ROWS, SCAN_ROWS)
        A, B = _scan_chunk(a1_ref[pl.ds(r0, SCAN_ROWS), :], b1_ref[pl.ds(r0, SCAN_ROWS), :], True)
        hh = A * h + B
        b1_ref[pl.ds(r0, SCAN_ROWS), :] = hh
        return hh[0:1, :]

    lax.fori_loop(0, nscan, bwd, jnp.zeros((1, BLOCK_W), F32))

    def fwd(j, h):
        r0 = pl.multiple_of(j * SCAN_ROWS, SCAN_ROWS)
        A, B = _scan_chunk(a0_ref[pl.ds(r0, SCAN_ROWS), :], b0_ref[pl.ds(r0, SCAN_ROWS), :], False)
        hh = A * h + B
        ga = ga_ref[pl.ds(r0, SCAN_ROWS), :].astype(F32)
        o_ref[pl.ds(r0, SCAN_ROWS), :] = ((hh + b1_ref[pl.ds(r0, SCAN_ROWS), :]) * _gelu_tanh(ga)).astype(BF16)
        return hh[SCAN_ROWS - 1:SCAN_ROWS, :]

    lax.fori_loop(0, nscan, fwd, jnp.zeros((1, BLOCK_W), F32))


def _rglru(z, conv_w, conv_b, w4, b4, lam4, batch, seq):
    n = batch * seq
    tc = _tile(seq, 256)
    xa_blk = Z_XA // BLOCK_W
    ga_blk = Z_GA // BLOCK_W
    return pl.pallas_call(
        functools.partial(_rglru_body, seq=seq, tc=tc),
        grid=(batch, RNN_BLOCKS),
        in_specs=[
            pl.BlockSpec((seq, BLOCK_W), lambda b, c: (b, xa_blk + c)),
            pl.BlockSpec((seq, BLOCK_W), lambda b, c: (b, ga_blk + c)),
            pl.BlockSpec((4, BLOCK_W), lambda b, c: (0, c)),
            pl.BlockSpec((1, BLOCK_W), lambda b, c: (0, c)),
            pl.BlockSpec((1, BLOCK_W, 4 * BLOCK_W), lambda b, c: (c, 0, 0)),
            pl.BlockSpec((1, 1, 4 * BLOCK_W), lambda b, c: (c, 0, 0)),
            pl.BlockSpec((1, 1, 2 * BLOCK_W), lambda b, c: (c, 0, 0)),
        ],
        out_specs=pl.BlockSpec((seq, BLOCK_W), lambda b, c: (b, c)),
        out_shape=jax.ShapeDtypeStruct((n, D_RNN), BF16),
        scratch_shapes=[pltpu.VMEM((seq, BLOCK_W), F32) for _ in range(4)],
        compiler_params=_cparams(("arbitrary", "arbitrary")),
        name="rglru",
    )(z, z, conv_w, conv_b, w4, b4, lam4)


KV_CHUNK = 512


def _qkv_body(cq_ref, ckv_ref, zk_ref, ct_ref, st_ref, qg_ref, kvg_ref,
              wqn_ref, wqr_ref, wk_ref, wvt_ref,
              qn_ref, qr_ref, kn_ref, kr_ref, vt_ref):
    scale = math.log2(math.e) / math.sqrt(QK_HEAD)
    ct = ct_ref[...]
    st = st_ref[...]
    cqn = _rms(cq_ref[...].astype(F32), qg_ref[...]).astype(BF16)
    ckvn = _rms(ckv_ref[...].astype(F32), kvg_ref[...]).astype(BF16)
    qn = jnp.dot(cqn, wqn_ref[...], preferred_element_type=F32)
    qn_ref[...] = (qn * scale).astype(BF16)
    qp = jnp.dot(cqn, wqr_ref[...], preferred_element_type=F32)
    cts = ct * scale
    sts = st * scale
    for h in range(N_HEADS):
        p = qp[:, h * LANES:(h + 1) * LANES]
        qr_ref[:, h * LANES:(h + 1) * LANES] = (p * cts + pltpu.roll(p, LANES // 2, axis=1) * sts).astype(BF16)
    kn_ref[...] = jnp.dot(ckvn, wk_ref[...], preferred_element_type=F32).astype(BF16)
    zk = zk_ref[...].astype(F32)
    kr_ref[...] = (zk * ct + pltpu.roll(zk, LANES // 2, axis=1) * st).astype(BF16)
    vt = lax.dot_general(wvt_ref[...], ckvn, (((1,), (1,)), ((), ())), preferred_element_type=F32)
    vt_ref[0] = vt.astype(BF16)


def _qkv(z, zk, ctab, stab, qg, kvg, wqn, wqr, wk, wvt, seq):
    n = z.shape[0]
    t = _tile(seq, KV_CHUNK)
    spt = seq // t
    cq_blk = Z_CQ // Q_LORA
    ckv_blk = Z_CKV // KV_LORA
    hd = N_HEADS * LANES
    const = lambda i: (0, 0)
    return pl.pallas_call(
        _qkv_body,
        grid=(n // t,),
        in_specs=[
            pl.BlockSpec((t, Q_LORA), lambda i: (i, cq_blk)),
            pl.BlockSpec((t, KV_LORA), lambda i: (i, ckv_blk)),
            pl.BlockSpec((t, LANES), lambda i: (i, 0)),
            pl.BlockSpec((t, LANES), lambda i: (i % spt, 0)),
            pl.BlockSpec((t, LANES), lambda i: (i % spt, 0)),
            pl.BlockSpec((1, Q_LORA), const),
            pl.BlockSpec((1, KV_LORA), const),
            pl.BlockSpec((Q_LORA, hd), const),
            pl.BlockSpec((Q_LORA, hd), const),
            pl.BlockSpec((KV_LORA, hd), const),
            pl.BlockSpec((hd, KV_LORA), const),
        ],
        out_specs=[
            pl.BlockSpec((t, hd), lambda i: (i, 0)),
            pl.BlockSpec((t, hd), lambda i: (i, 0)),
            pl.BlockSpec((t, hd), lambda i: (i, 0)),
            pl.BlockSpec((t, LANES), lambda i: (i, 0)),
            pl.BlockSpec((1, hd, t), lambda i: (i, 0, 0)),
        ],
        out_shape=[
            jax.ShapeDtypeStruct((n, hd), BF16),
            jax.ShapeDtypeStruct((n, hd), BF16),
            jax.ShapeDtypeStruct((n, hd), BF16),
            jax.ShapeDtypeStruct((n, LANES), BF16),
            jax.ShapeDtypeStruct((n // t, hd, t), BF16),
        ],
        compiler_params=_cparams(("arbitrary",)),
        name="qkv",
    )(z, z, zk, ctab, stab, qg, kvg, wqn, wqr, wk, wvt)


ATTN_CHAINS = 4
ONES_PAD = 16


def _attn_body(qn_ref, qr_ref, kn_ref, kr_ref, vt_ref, o_ref, *, nk, kc):
    tq = qn_ref.shape[0]
    th = tq // ATTN_CHAINS
    qs = [jnp.concatenate([qn_ref[h * th:(h + 1) * th, :], qr_ref[h * th:(h + 1) * th, :]], axis=1)
          for h in range(ATTN_CHAINS)]

    def keys(c):
        return jnp.concatenate([kn_ref[c * kc:(c + 1) * kc, :], kr_ref[c * kc:(c + 1) * kc, :]], axis=1)

    def scores(k, h):
        return lax.dot_general(k, qs[h], (((1,), (1,)), ((), ())), preferred_element_type=F32)

    ones_rows = jnp.where(lax.broadcasted_iota(I32, (ONES_PAD, kc), 0) == 0, 1.0, 0.0).astype(BF16)
    m = [jnp.full((1, th), NEG_BIG, F32) for _ in range(ATTN_CHAINS)]
    acc = [jnp.zeros((V_HEAD + ONES_PAD, th), F32) for _ in range(ATTN_CHAINS)]
    k_first = keys(0)
    s_next = [scores(k_first, h) for h in range(ATTN_CHAINS)]
    for c in range(nk):
        s_cur = s_next
        if c + 1 < nk:
            k_ahead = keys(c + 1)
            s_next = [scores(k_ahead, h) for h in range(ATTN_CHAINS)]
        v_ext = jnp.concatenate([vt_ref[c], ones_rows], axis=0)
        for h in range(ATTN_CHAINS):
            s = s_cur[h]
            m_new = jnp.maximum(m[h], jnp.max(s, axis=0, keepdims=True))
            alpha = jnp.exp2(m[h] - m_new)
            p = jnp.exp2((s - m_new).astype(BF16))
            acc[h] = alpha * acc[h] + jnp.dot(v_ext, p, preferred_element_type=F32)
            m[h] = m_new
    for h in range(ATTN_CHAINS):
        o = acc[h][:V_HEAD] * (1.0 / acc[h][V_HEAD:V_HEAD + 1])
        o_ref[h * th:(h + 1) * th, :] = o.T.astype(BF16)


def _attention(qn, qr, kn, kr, vt, batch, seq):
    n = batch * seq
    tq = _tile(seq, 2048)
    kc = _tile(seq, KV_CHUNK)
    nq = seq // tq
    nk = seq // kc
    return pl.pallas_call(
        functools.partial(_attn_body, nk=nk, kc=kc),
        grid=(batch, N_HEADS, nq),
        in_specs=[
            pl.BlockSpec((tq, LANES), lambda b, h, i: (b * nq + i, h)),
            pl.BlockSpec((tq, LANES), lambda b, h, i: (b * nq + i, h)),
            pl.BlockSpec((seq, LANES), lambda b, h, i: (b, h)),
            pl.BlockSpec((seq, LANES), lambda b, h, i: (b, 0)),
            pl.BlockSpec((nk, V_HEAD, kc), lambda b, h, i: (b, h, 0)),
        ],
        out_specs=pl.BlockSpec((tq, V_HEAD), lambda b, h, i: (b * nq + i, h)),
        out_shape=jax.ShapeDtypeStruct((n, N_HEADS * V_HEAD), BF16),
        compiler_params=_cparams(("arbitrary", "arbitrary", "arbitrary")),
        name="attn",
    )(qn, qr, kn, kr, vt)


def _outproj_body(x_ref, hg_ref, at_ref, gla_ref, glb_ref, wa_ref, wb_ref, wo_ref, g_ref, wr_ref, br_ref,
                  x1_ref, idx_ref, gate_ref, rank_ref, cnt_ref, carry_ref):
    t = x_ref.shape[0]

    @pl.when(pl.program_id(0) == 0)
    def _():
        carry_ref[...] = jnp.zeros(carry_ref.shape, F32)

    ya = jnp.dot(hg_ref[...], wa_ref[...], preferred_element_type=F32)
    yb = jnp.dot(at_ref[...], wb_ref[...], preferred_element_type=F32)
    m = _sigmoid(gla_ref[...].astype(F32)) * ya + _sigmoid(glb_ref[...].astype(F32)) * yb
    x1 = x_ref[...] + jnp.dot(m.astype(BF16), wo_ref[...], preferred_element_type=F32)
    x1_ref[...] = x1
    un = _rms(x1, g_ref[...])
    un_hi = un.astype(BF16)
    un_lo = (un - un_hi.astype(F32)).astype(BF16)
    w_hi = wr_ref[0]
    logits = (jnp.dot(un_hi, w_hi, preferred_element_type=F32)
              + jnp.dot(un_lo, w_hi, preferred_element_type=F32)
              + jnp.dot(un_hi, wr_ref[1], preferred_element_type=F32)).T + br_ref[...]
    e_iota = lax.broadcasted_iota(I32, logits.shape, 0)
    vals, idxs, sels = [], [], []
    cur = logits
    for _ in range(TOP_K):
        mx = jnp.max(cur, axis=0, keepdims=True)
        idx = jnp.min(jnp.where(cur == mx, e_iota, N_EXPERTS), axis=0, keepdims=True)
        sel = e_iota == idx
        vals.append(mx)
        idxs.append(idx)
        sels.append(sel)
        cur = jnp.where(sel, -jnp.inf, cur)
    ex = [jnp.exp(v - vals[0]) for v in vals]
    inv = 1.0 / (ex[0] + ex[1] + ex[2] + ex[3])
    onehot = jnp.where(sels[0] | sels[1] | sels[2] | sels[3], 1.0, 0.0)
    row = lax.broadcasted_iota(I32, (t, t), 0)
    col = lax.broadcasted_iota(I32, (t, t), 1)
    upper = jnp.where(row < col, 1.0, 0.0).astype(BF16)
    prefix = jnp.dot(onehot.astype(BF16), upper, preferred_element_type=F32) + carry_ref[:, 0:1]
    for k in range(TOP_K):
        idx_ref[k:k + 1, :] = idxs[k]
        gate_ref[k:k + 1, :] = ex[k] * inv
        rank_ref[k:k + 1, :] = jnp.sum(jnp.where(sels[k], prefix, 0.0), axis=0, keepdims=True).astype(I32)
    gate_ref[TOP_K:, :] = jnp.zeros((gate_ref.shape[0] - TOP_K, t), F32)
    carry_ref[...] = carry_ref[...] + jnp.sum(onehot, axis=1, keepdims=True)
    cnt_ref[...] = carry_ref[...]


def _outproj(x2d, hg, attn, z, wa, wb, wo, g, wr2, br):
    n = x2d.shape[0]
    t = _tile(n, 256)
    const = lambda i: (0, 0)
    once = pl.Buffered(1)
    return pl.pallas_call(
        _outproj_body,
        grid=(n // t,),
        in_specs=[
            pl.BlockSpec((t, D_MODEL), lambda i: (i, 0)),
            pl.BlockSpec((t, D_RNN), lambda i: (i, 0)),
            pl.BlockSpec((t, D_MODEL), lambda i: (i, 0)),
            pl.BlockSpec((t, D_MODEL), lambda i: (i, Z_GLA // D_MODEL)),
            pl.BlockSpec((t, D_MODEL), lambda i: (i, Z_GLB // D_MODEL)),
            pl.BlockSpec((D_RNN, D_MODEL), const, pipeline_mode=once),
            pl.BlockSpec((D_MODEL, D_MODEL), const, pipeline_mode=once),
            pl.BlockSpec((D_MODEL, D_MODEL), const, pipeline_mode=once),
            pl.BlockSpec((1, D_MODEL), const),
            pl.BlockSpec((2, D_MODEL, N_EXPERTS), lambda i: (0, 0, 0)),
            pl.BlockSpec((N_EXPERTS, 1), const),
        ],
        out_specs=[
            pl.BlockSpec((t, D_MODEL), lambda i: (i, 0)),
            pl.BlockSpec((TOP_K, t), lambda i: (0, i)),
            pl.BlockSpec((8, t), lambda i: (0, i)),
            pl.BlockSpec((TOP_K, t), lambda i: (0, i)),
            pl.BlockSpec((N_EXPERTS, LANES), const),
        ],
        out_shape=[
            jax.ShapeDtypeStruct((n, D_MODEL), F32),
            jax.ShapeDtypeStruct((TOP_K, n), I32),
            jax.ShapeDtypeStruct((8, n), F32),
            jax.ShapeDtypeStruct((TOP_K, n), I32),
            jax.ShapeDtypeStruct((N_EXPERTS, LANES), F32),
        ],
        scratch_shapes=[pltpu.VMEM((N_EXPERTS, LANES), F32)],
        compiler_params=_cparams(("arbitrary",)),
        name="outproj",
    )(x2d, hg, attn, z, z, wa, wb, wo, g, wr2, br)


ZERO_ROWS = 256


def _dispatch_body(fs_ref, fe_ref, nu_ref, dest_ref, x1_ref, g_ref, xs_ref, un_ref, zero_ref, sem, zsem,
                   *, nsteps, nblocks):
    t = x1_ref.shape[0]
    i = pl.program_id(0)
    slot = i % 2

    def wait_slot(s):
        for _ in range(TOP_K):
            pltpu.make_async_copy(un_ref.at[s], xs_ref.at[pl.ds(0, t), :], sem.at[s]).wait()

    @pl.when(i >= 2)
    def _():
        wait_slot(slot)

    un_ref[slot] = _rms(x1_ref[...], g_ref[...])

    def issue(r, carry):
        for k in range(TOP_K):
            pltpu.make_async_copy(un_ref.at[slot, pl.ds(r, 1), :],
                                  xs_ref.at[pl.ds(dest_ref[k, r], 1), :], sem.at[slot]).start()
        return carry

    lax.fori_loop(0, t, issue, 0, unroll=8)

    @pl.when(i == 0)
    def _():
        zero_ref[...] = jnp.zeros(zero_ref.shape, F32)

        def fill_expert(e, carry):
            def start(r, c):
                pltpu.make_async_copy(zero_ref.at[pl.ds(0, 1), :], xs_ref.at[pl.ds(r, 1), :], zsem).start()
                return c

            def wait(r, c):
                pltpu.make_async_copy(zero_ref.at[pl.ds(0, 1), :], xs_ref.at[pl.ds(0, 1), :], zsem).wait()
                return c

            lax.fori_loop(fs_ref[e], fe_ref[e], start, 0)
            lax.fori_loop(fs_ref[e], fe_ref[e], wait, 0)
            return carry

        lax.fori_loop(0, N_EXPERTS, fill_expert, 0)

        def fill_block(b, carry):
            for part in range(MOE_ROWS // ZERO_ROWS):
                r0 = pl.multiple_of(b * MOE_ROWS + part * ZERO_ROWS, ZERO_ROWS)
                cp = pltpu.make_async_copy(zero_ref, xs_ref.at[pl.ds(r0, ZERO_ROWS), :], zsem)
                cp.start()
                cp.wait()
            return carry

        lax.fori_loop(nu_ref[0], nblocks, fill_block, 0)

    @pl.when(i == nsteps - 1)
    def _():
        wait_slot(slot)
        if nsteps >= 2:
            wait_slot(1 - slot)


def _dispatch(fill_start, fill_end, n_used, dest, x1, g, rows):
    n = x1.shape[0]
    t = _tile(n, 256)
    nsteps = n // t
    grid_spec = pltpu.PrefetchScalarGridSpec(
        num_scalar_prefetch=3,
        grid=(nsteps,),
        in_specs=[
            pl.BlockSpec((TOP_K, t), lambda i, fs, fe, nu: (0, i), memory_space=pltpu.SMEM),
            pl.BlockSpec((t, D_MODEL), lambda i, fs, fe, nu: (i, 0)),
            pl.BlockSpec((1, D_MODEL), lambda i, fs, fe, nu: (0, 0)),
        ],
        out_specs=pl.BlockSpec(memory_space=pl.ANY),
        scratch_shapes=[pltpu.VMEM((2, t, D_MODEL), F32), pltpu.VMEM((ZERO_ROWS, D_MODEL), F32),
                        pltpu.SemaphoreType.DMA((2,)), pltpu.SemaphoreType.DMA(())],
    )
    return pl.pallas_call(
        functools.partial(_dispatch_body, nsteps=nsteps, nblocks=rows // MOE_ROWS),
        grid_spec=grid_spec,
        out_shape=jax.ShapeDtypeStruct((rows, D_MODEL), F32),
        compiler_params=_cparams(("arbitrary",)),
        name="dispatch",
    )(fill_start, fill_end, n_used, dest, x1, g)


MOE_ROWS = 512
MOE_FF = 1024


def _experts_body(be_ref, nu_ref, xs_ref, wg_ref, wu_ref, bg_ref, bu_ref, wd_ref, bd_ref,
                  ys_ref, xb_ref):
    i = pl.program_id(0)
    f = pl.program_id(1)

    @pl.when(jnp.logical_and(i >= nu_ref[0], f == 0))
    def _():
        ys_ref[...] = jnp.zeros(ys_ref.shape, F32)

    @pl.when(i < nu_ref[0])
    def _():
        @pl.when(f == 0)
        def _():
            xb_ref[...] = xs_ref[...].astype(BF16)

        xb = xb_ref[...]
        nt = (((1,), (1,)), ((), ()))
        g = lax.dot_general(xb, wg_ref[0], nt, preferred_element_type=F32) + bg_ref[0]
        u = lax.dot_general(xb, wu_ref[0], nt, preferred_element_type=F32) + bu_ref[0]
        g = jnp.minimum(g, SWIGLU_LIMIT)
        u = jnp.clip(u, -SWIGLU_LIMIT, SWIGLU_LIMIT)
        h = (g * _sigmoid(SWIGLU_ALPHA * g) * (u + 1.0)).astype(BF16)
        y = jnp.dot(h, wd_ref[0], preferred_element_type=F32)

        @pl.when(f == 0)
        def _():
            ys_ref[...] = y + bd_ref[0]

        @pl.when(f > 0)
        def _():
            ys_ref[...] = ys_ref[...] + y


def _experts(block_e, n_used, xs, wg_t, wu_t, bg, bu, wd, bd):
    rows = xs.shape[0]
    tm = MOE_ROWS
    nb = rows // tm
    nf = D_FF // MOE_FF

    def blk(i, nu):
        return jnp.minimum(i, nu[0] - 1)

    def fidx(i, f, nu):
        return jnp.where(i < nu[0], f, nf - 1)

    grid_spec = pltpu.PrefetchScalarGridSpec(
        num_scalar_prefetch=2,
        grid=(nb, nf),
        in_specs=[
            pl.BlockSpec((tm, D_MODEL), lambda i, f, be, nu: (blk(i, nu), 0)),
            pl.BlockSpec((1, MOE_FF, D_MODEL), lambda i, f, be, nu: (be[blk(i, nu)], fidx(i, f, nu), 0)),
            pl.BlockSpec((1, MOE_FF, D_MODEL), lambda i, f, be, nu: (be[blk(i, nu)], fidx(i, f, nu), 0)),
            pl.BlockSpec((1, 1, MOE_FF), lambda i, f, be, nu: (be[blk(i, nu)], 0, fidx(i, f, nu))),
            pl.BlockSpec((1, 1, MOE_FF), lambda i, f, be, nu: (be[blk(i, nu)], 0, fidx(i, f, nu))),
            pl.BlockSpec((1, MOE_FF, D_MODEL), lambda i, f, be, nu: (be[blk(i, nu)], fidx(i, f, nu), 0)),
            pl.BlockSpec((1, 1, D_MODEL), lambda i, f, be, nu: (be[blk(i, nu)], 0, 0)),
        ],
        out_specs=pl.BlockSpec((tm, D_MODEL), lambda i, f, be, nu: (i, 0)),
        scratch_shapes=[pltpu.VMEM((tm, D_MODEL), BF16)],
    )
    return pl.pallas_call(
        _experts_body,
        grid_spec=grid_spec,
        out_shape=jax.ShapeDtypeStruct((rows, D_MODEL), F32),
        compiler_params=_cparams(("arbitrary", "arbitrary")),
        name="experts",
    )(block_e, n_used, xs, wg_t, wu_t, bg, bu, wd, bd)


def _combine_body(dest_ref, x1_ref, gate_ref, g_ref, ys_ref, o_ref, buf_ref, sem):
    t = x1_ref.shape[0]

    def issue(r, carry):
        for k in range(TOP_K):
            pltpu.make_async_copy(ys_ref.at[pl.ds(dest_ref[k, r], 1), :],
                                  buf_ref.at[k, pl.ds(r, 1), :], sem).start()
        return carry

    lax.fori_loop(0, t, issue, 0, unroll=8)
    for k in range(TOP_K):
        pltpu.make_async_copy(ys_ref.at[pl.ds(0, t), :], buf_ref.at[k], sem).wait()
    gcol = gate_ref[...].T
    y = x1_ref[...]
    for k in range(TOP_K):
        y = y + gcol[:, k:k + 1] * buf_ref[k]
    o_ref[...] = _rms(y, g_ref[...])


def _combine(dest, x1, gate, g, ys):
    n = x1.shape[0]
    t = _tile(n, 256)
    return pl.pallas_call(
        _combine_body,
        grid=(n // t,),
        in_specs=[
            pl.BlockSpec((TOP_K, t), lambda i: (0, i), memory_space=pltpu.SMEM),
            pl.BlockSpec((t, D_MODEL), lambda i: (i, 0)),
            pl.BlockSpec((8, t), lambda i: (0, i)),
            pl.BlockSpec((1, D_MODEL), lambda i: (0, 0)),
            pl.BlockSpec(memory_space=pl.ANY),
        ],
        out_specs=pl.BlockSpec((t, D_MODEL), lambda i: (i, 0)),
        out_shape=jax.ShapeDtypeStruct((n, D_MODEL), F32),
        scratch_shapes=[pltpu.VMEM((TOP_K, t, D_MODEL), F32), pltpu.SemaphoreType.DMA(())],
        compiler_params=_cparams(("arbitrary",)),
        name="combine",
    )(dest, x1, gate, g, ys)


PREP_K = 512


def _prep_gate_up_body(w_ref, g_ref, u_ref, t_ref):
    wt = w_ref[0].T
    for c in range(PREP_K // LANES):
        cols = slice(c * LANES, (c + 1) * LANES)
        t_ref[c] = wt[:, cols]
        g_ref[0, :, cols] = t_ref[c, pl.ds(0, D_FF, stride=2), :].astype(BF16)
        u_ref[0, :, cols] = t_ref[c, pl.ds(1, D_FF, stride=2), :].astype(BF16)


def _prep_gate_up(w_gate_up):
    out = jax.ShapeDtypeStruct((N_EXPERTS, D_FF, D_MODEL), BF16)
    return pl.pallas_call(
        _prep_gate_up_body,
        grid=(N_EXPERTS, D_MODEL // PREP_K),
        in_specs=[pl.BlockSpec((1, PREP_K, 2 * D_FF), lambda e, k: (e, k, 0))],
        out_specs=[pl.BlockSpec((1, D_FF, PREP_K), lambda e, k: (e, 0, k)),
                   pl.BlockSpec((1, D_FF, PREP_K), lambda e, k: (e, 0, k))],
        out_shape=[out, out],
        scratch_shapes=[pltpu.VMEM((PREP_K // LANES, 2 * D_FF, LANES), F32)],
        compiler_params=_cparams(("arbitrary", "arbitrary")),
        name="prep_gate_up",
    )(w_gate_up)


def _cast_body(w_ref, o_ref):
    o_ref[...] = w_ref[...].astype(BF16)


def _prep_cast(w):
    e, r, c = w.shape
    tr = _tile(r, 1024)
    return pl.pallas_call(
        _cast_body,
        grid=(e, r // tr),
        in_specs=[pl.BlockSpec((1, tr, c), lambda i, j: (i, j, 0))],
        out_specs=pl.BlockSpec((1, tr, c), lambda i, j: (i, j, 0)),
        out_shape=jax.ShapeDtypeStruct(w.shape, BF16),
        compiler_params=_cparams(("arbitrary", "arbitrary")),
        name="prep_cast",
    )(w)


def _rot_cols(w):
    half = QK_ROPE // 2
    return jnp.concatenate([-w[..., half:], w[..., :half]], axis=-1)


def _prep_weights(norm_mix_g, w_in, conv_w, conv_b, rg_w_a, rg_b_a, rg_w_x, rg_b_x, rg_lambda,
                  w_proj_a, q_norm_g, w_uq, kv_norm_g, w_uk, w_uv, w_proj_b, w_out,
                  norm_ffn_g, w_router, b_router, w_gate_up, b_gate_up, w_down, b_down, norm_final_g):
    p = {}
    wi = w_in[0]
    o_xa, o_ga, o_cq, o_ckv, o_kr, o_gla, o_glb = 0, 1024, 2048, 2560, 3072, 3136, 5184
    p["w_main"] = jnp.concatenate([
        wi[:, o_gla:o_gla + D_MODEL], wi[:, o_glb:o_glb + D_MODEL],
        wi[:, o_xa:o_xa + D_RNN], wi[:, o_ga:o_ga + D_RNN],
        wi[:, o_cq:o_cq + Q_LORA], wi[:, o_ckv:o_ckv + KV_LORA]], axis=1).astype(BF16)
    wkr = wi[:, o_kr:o_kr + QK_ROPE]
    p["w_k2"] = jnp.concatenate([wkr, _rot_cols(wkr)], axis=1).astype(BF16)
    p["g_mix"] = norm_mix_g[0].reshape(1, D_MODEL)
    p["conv_w"] = conv_w[0]
    p["conv_b"] = conv_b[0].reshape(1, D_RNN)
    wa, wx = rg_w_a[0], rg_w_x[0]
    p["w4"] = jnp.concatenate([wa[0], wx[0], wa[1], wx[1]], axis=-1).astype(BF16)
    ba = rg_b_a[0].reshape(2, RNN_BLOCKS, 1, BLOCK_W)
    bx = rg_b_x[0].reshape(2, RNN_BLOCKS, 1, BLOCK_W)
    p["b4"] = jnp.concatenate([ba[0], bx[0], ba[1], bx[1]], axis=-1)
    lam = rg_lambda[0].reshape(2, RNN_BLOCKS, 1, BLOCK_W)
    p["lam4"] = jnp.concatenate([lam[0], lam[1]], axis=-1)
    p["w_a"] = w_proj_a[0].astype(BF16)
    p["q_g"] = q_norm_g[0].reshape(1, Q_LORA)
    p["kv_g"] = kv_norm_g[0].reshape(1, KV_LORA)
    wq = w_uq[0].reshape(Q_LORA, N_HEADS, QK_HEAD)
    p["wq_n"] = wq[:, :, :QK_NOPE].reshape(Q_LORA, N_HEADS * QK_NOPE).astype(BF16)
    wqr = wq[:, :, QK_NOPE:]
    p["wq_r"] = jnp.concatenate([wqr, _rot_cols(wqr)], axis=-1).reshape(Q_LORA, N_HEADS * LANES).astype(BF16)
    p["w_k"] = w_uk[0].astype(BF16)
    p["w_vt"] = w_uv[0].T.astype(BF16)
    p["w_b"] = w_proj_b[0].astype(BF16)
    p["w_o"] = w_out[0].astype(BF16)
    p["g_ffn"] = norm_ffn_g[0].reshape(1, D_MODEL)
    w_r_hi = w_router[0].astype(BF16)
    w_r_lo = (w_router[0] - w_r_hi.astype(F32)).astype(BF16)
    p["w_r2"] = jnp.stack([w_r_hi, w_r_lo])
    p["b_r"] = b_router[0].reshape(N_EXPERTS, 1)
    p["w_g_t"], p["w_u_t"] = _prep_gate_up(w_gate_up[0])
    bgu = b_gate_up[0]
    p["b_g"] = bgu[:, 0::2].reshape(N_EXPERTS, 1, D_FF)
    p["b_u"] = bgu[:, 1::2].reshape(N_EXPERTS, 1, D_FF)
    p["w_d"] = _prep_cast(w_down[0])
    p["b_d"] = b_down[0].reshape(N_EXPERTS, 1, D_MODEL)
    p["g_fin"] = norm_final_g.reshape(1, D_MODEL)
    return p


def _rope_tables(seq):
    half = QK_ROPE // 2
    freqs = ROPE_THETA ** (-jnp.arange(half, dtype=F32) / half)
    ang = jnp.arange(seq, dtype=F32)[:, None] * freqs[None, :]
    zero = jnp.zeros((seq, LANES - QK_ROPE), F32)
    ctab = jnp.concatenate([jnp.cos(ang), jnp.cos(ang), zero], axis=1)
    stab = jnp.concatenate([jnp.sin(ang), jnp.sin(ang), zero], axis=1)
    return ctab, stab


def _trunk(x, p):
    batch, seq, _ = x.shape
    n = batch * seq
    x2d = x.reshape(n, D_MODEL)
    z, zk = _in_proj(x2d, p["g_mix"], p["w_main"], p["w_k2"])
    hg = _rglru(z, p["conv_w"], p["conv_b"], p["w4"], p["b4"], p["lam4"], batch, seq)
    ctab, stab = _rope_tables(seq)
    qn, qr, kn, kr, vt = _qkv(z, zk, ctab, stab, p["q_g"], p["kv_g"],
                              p["wq_n"], p["wq_r"], p["w_k"], p["w_vt"], seq)
    attn = _attention(qn, qr, kn, kr, vt, batch, seq)
    x1, idx, gate, rank, cnt = _outproj(x2d, hg, attn, z, p["w_a"], p["w_b"], p["w_o"],
                                        p["g_ffn"], p["w_r2"], p["b_r"])

    tm = MOE_ROWS
    counts = cnt[:, 0].astype(I32)
    padded = (counts + tm - 1) // tm * tm
    pad_end = jnp.cumsum(padded)
    pad_start = pad_end - padded
    experts = jnp.arange(N_EXPERTS, dtype=I32)
    dest = jnp.sum(jnp.where(idx[:, :, None] == experts, pad_start, 0), axis=-1) + rank
    nb = -(-(n * TOP_K) // tm) + N_EXPERTS
    rows = nb * tm
    blk_start = jnp.arange(nb, dtype=I32) * tm
    block_e = jnp.minimum(jnp.sum((pad_end[None, :] <= blk_start[:, None]).astype(I32), axis=1), N_EXPERTS - 1)
    n_used = (pad_end[-1:] // tm).astype(I32)

    xs = _dispatch(pad_start + counts, pad_end, n_used, dest, x1, p["g_ffn"], rows)
    ys = _experts(block_e, n_used, xs, p["w_g_t"], p["w_u_t"], p["b_g"], p["b_u"], p["w_d"], p["b_d"])
    out = _combine(dest, x1, gate, p["g_fin"], ys)
    return out.reshape(batch, seq, D_MODEL)


def kernel(x_prompt, x_sample, norm_mix_g, w_in, conv_w, conv_b, rg_w_a, rg_b_a, rg_w_x, rg_b_x, rg_lambda,
           w_proj_a, q_norm_g, w_uq, kv_norm_g, w_uk, w_uv, w_proj_b, w_out, norm_ffn_g, w_router, b_router,
           w_gate_up, b_gate_up, w_down, b_down, norm_final_g):
    p = _prep_weights(norm_mix_g, w_in, conv_w, conv_b, rg_w_a, rg_b_a, rg_w_x, rg_b_x, rg_lambda,
                      w_proj_a, q_norm_g, w_uq, kv_norm_g, w_uk, w_uv, w_proj_b, w_out,
                      norm_ffn_g, w_router, b_router, w_gate_up, b_gate_up, w_down, b_down, norm_final_g)
    return (_trunk(x_prompt, p), _trunk(x_sample, p))
```

```python
import functools
import math

import jax
import jax.numpy as jnp
from jax import lax
from jax.experimental import pallas as pl
from jax.experimental.pallas import tpu as pltpu

F32 = jnp.float32
BF16 = jnp.bfloat16
I32 = jnp.int32

D_MODEL = 2048
D_RNN = 1024
RNN_BLOCKS = 8
BLOCK_W = 128
C_RG = 8.0
N_HEADS = 16
QK_NOPE = 128
QK_ROPE = 64
QK_HEAD = QK_NOPE + QK_ROPE
V_HEAD = 128
Q_LORA = 512
KV_LORA = 512
ROPE_THETA = 10000.0
N_EXPERTS = 32
TOP_K = 4
D_FF = 2048
SWIGLU_LIMIT = 7.0
SWIGLU_ALPHA = 1.702
EPS = 1e-6

LANES = 128
VMEM_LIMIT = 56 * 1024 * 1024
NEG_BIG = -1e30

Z_GLA, Z_GLB, Z_XA, Z_GA, Z_CQ, Z_CKV = 0, 2048, 4096, 5120, 6144, 6656
Z_COLS = 7168


def _cparams(sem):
    return pltpu.CompilerParams(dimension_semantics=sem, vmem_limit_bytes=VMEM_LIMIT)


def _tile(n, pref):
    t = min(n, pref)
    assert n % t == 0, (n, pref)
    return t


def _rms(x, g):
    return x * lax.rsqrt(jnp.mean(x * x, axis=-1, keepdims=True) + EPS) * g


def _in_proj_body(x_ref, g_ref, w_ref, wk_ref, z_ref, zk_ref, u_ref):
    @pl.when(pl.program_id(1) == 0)
    def _():
        u = _rms(x_ref[...], g_ref[...]).astype(BF16)
        u_ref[...] = u
        zk_ref[...] = jnp.dot(u, wk_ref[...], preferred_element_type=F32).astype(BF16)

    z_ref[...] = jnp.dot(u_ref[...], w_ref[...], preferred_element_type=F32).astype(BF16)


def _in_proj(x2d, g, w_main, w_k2):
    n = x2d.shape[0]
    tm = _tile(n, 1024)
    tn = 1024
    return pl.pallas_call(
        _in_proj_body,
        grid=(n // tm, Z_COLS // tn),
        in_specs=[
            pl.BlockSpec((tm, D_MODEL), lambda i, j: (i, 0)),
            pl.BlockSpec((1, D_MODEL), lambda i, j: (0, 0)),
            pl.BlockSpec((D_MODEL, tn), lambda i, j: (0, j)),
            pl.BlockSpec((D_MODEL, LANES), lambda i, j: (0, 0)),
        ],
        out_specs=[
            pl.BlockSpec((tm, tn), lambda i, j: (i, j)),
            pl.BlockSpec((tm, LANES), lambda i, j: (i, 0)),
        ],
        out_shape=[
            jax.ShapeDtypeStruct((n, Z_COLS), BF16),
            jax.ShapeDtypeStruct((n, LANES), BF16),
        ],
        scratch_shapes=[pltpu.VMEM((tm, D_MODEL), BF16)],
        compiler_params=_cparams(("arbitrary", "arbitrary")),
        name="in_proj",
    )(x2d, g, w_main, w_k2)


HALO = 16
SCAN_ROWS = 128


def _sigmoid(x):
    return 0.5 * jnp.tanh(0.5 * x) + 0.5


def _gelu_tanh(x):
    return 0.5 * x * (1.0 + jnp.tanh(math.sqrt(2.0 / math.pi) * (x + 0.044715 * (x * x * x))))


def _scan_chunk(a, b, reverse):
    rows = a.shape[0]
    row = lax.broadcasted_iota(I32, a.shape, 0)
    d = 1
    while d < rows:
        if d < 8:
            shift = (rows - d) if reverse else d
            a_sh = pltpu.roll(a, shift, axis=0)
            b_sh = pltpu.roll(b, shift, axis=0)
            keep = (row < rows - d) if reverse else (row >= d)
            a_sh = jnp.where(keep, a_sh, 1.0)
            b_sh = jnp.where(keep, b_sh, 0.0)
        else:
            one = jnp.ones((d, a.shape[1]), F32)
            zero = jnp.zeros((d, a.shape[1]), F32)
            if reverse:
                a_sh = jnp.concatenate([a[d:], one], axis=0)
                b_sh = jnp.concatenate([b[d:], zero], axis=0)
            else:
                a_sh = jnp.concatenate([one, a[:-d]], axis=0)
                b_sh = jnp.concatenate([zero, b[:-d]], axis=0)
        b = a * b_sh + b
        a = a * a_sh
        d *= 2
    return a, b


def _rglru_body(xa_ref, ga_ref, cw_ref, cb_ref, w4_ref, b4_ref, lam_ref, o_ref,
                a0_ref, b0_ref, a1_ref, b1_ref, *, seq, tc):
    nchunk = seq // tc
    lam = lam_ref[0]
    sp = jnp.maximum(-lam, 0.0) + jnp.log1p(jnp.exp(-jnp.abs(lam)))
    sp0 = sp[:, :BLOCK_W]
    sp1 = sp[:, BLOCK_W:]
    cw = cw_ref[...]
    cb = cb_ref[...]
    w4 = w4_ref[0]
    b4 = b4_ref[0]

    def gates(j, carry):
        r0 = pl.multiple_of(j * tc, tc)
        cur = xa_ref[pl.ds(r0, tc), :].astype(F32)
        prev_start = pl.multiple_of(jnp.maximum(r0 - HALO, 0), HALO)
        next_start = pl.multiple_of(jnp.minimum(r0 + tc, seq - HALO), HALO)
        prev = xa_ref[pl.ds(prev_start, HALO), :].astype(F32)
        nxt = xa_ref[pl.ds(next_start, HALO), :].astype(F32)
        prev = jnp.where(j > 0, prev, 0.0)
        nxt = jnp.where(j < nchunk - 1, nxt, 0.0)
        ext = jnp.concatenate([prev, cur, nxt], axis=0)
        n_ext = tc + 2 * HALO
        xm2 = pltpu.roll(ext, 2, axis=0)[HALO:HALO + tc]
        xm1 = pltpu.roll(ext, 1, axis=0)[HALO:HALO + tc]
        xp1 = pltpu.roll(ext, n_ext - 1, axis=0)[HALO:HALO + tc]
        xc = xm2 * cw[0:1] + xm1 * cw[1:2] + cur * cw[2:3] + xp1 * cw[3:4] + cb
        pre = jnp.dot(xc.astype(BF16), w4, preferred_element_type=F32) + b4
        for d, (a_ref, b_ref, spd) in enumerate(((a0_ref, b0_ref, sp0), (a1_ref, b1_ref, sp1))):
            r = _sigmoid(pre[:, (2 * d) * BLOCK_W:(2 * d + 1) * BLOCK_W])
            i = _sigmoid(pre[:, (2 * d + 1) * BLOCK_W:(2 * d + 2) * BLOCK_W])
            a = jnp.exp((-C_RG) * r * spd)
            b = jnp.sqrt(1.0 - a * a) * (i * xc)
            a_ref[pl.ds(r0, tc), :] = a
            b_ref[pl.ds(r0, tc), :] = b
        return carry

    lax.fori_loop(0, nchunk, gates, 0, unroll=2)

    nscan = seq // SCAN_ROWS

    def bwd(jj, h):
        j = nscan - 1 - jj
        r0 = pl.multiple_of(j * SCAN_ROWS, SCAN_ROWS)
        A, B = _scan_chunk(a1_ref[pl.ds(r0, SCAN_ROWS), :], b1_ref[pl.ds(r0, SCAN_ROWS), :], True)
        hh = A * h + B
        b1_ref[pl.ds(r0, SCAN_ROWS), :] = hh
        return hh[0:1, :]

    lax.fori_loop(0, nscan, bwd, jnp.zeros((1, BLOCK_W), F32))

    def fwd(j, h):
        r0 = pl.multiple_of(j * SCAN_ROWS, SCAN_ROWS)
        A, B = _scan_chunk(a0_ref[pl.ds(r0, SCAN_ROWS), :], b0_ref[pl.ds(r0, SCAN_ROWS), :], False)
        hh = A * h + B
        ga = ga_ref[pl.ds(r0, SCAN_ROWS), :].astype(F32)
        o_ref[pl.ds(r0, SCAN_ROWS), :] = ((hh + b1_ref[pl.ds(r0, SCAN_ROWS), :]) * _gelu_tanh(ga)).astype(BF16)
        return hh[SCAN_ROWS - 1:SCAN_ROWS, :]

    lax.fori_loop(0, nscan, fwd, jnp.zeros((1, BLOCK_W), F32))


def _rglru(z, conv_w, conv_b, w4, b4, lam4, batch, seq):
    n = batch * seq
    tc = _tile(seq, 256)
    xa_blk = Z_XA // BLOCK_W
    ga_blk = Z_GA // BLOCK_W
    return pl.pallas_call(
        functools.partial(_rglru_body, seq=seq, tc=tc),
        grid=(batch, RNN_BLOCKS),
        in_specs=[
            pl.BlockSpec((seq, BLOCK_W), lambda b, c: (b, xa_blk + c)),
            pl.BlockSpec((seq, BLOCK_W), lambda b, c: (b, ga_blk + c)),
            pl.BlockSpec((4, BLOCK_W), lambda b, c: (0, c)),
            pl.BlockSpec((1, BLOCK_W), lambda b, c: (0, c)),
            pl.BlockSpec((1, BLOCK_W, 4 * BLOCK_W), lambda b, c: (c, 0, 0)),
            pl.BlockSpec((1, 1, 4 * BLOCK_W), lambda b, c: (c, 0, 0)),
            pl.BlockSpec((1, 1, 2 * BLOCK_W), lambda b, c: (c, 0, 0)),
        ],
        out_specs=pl.BlockSpec((seq, BLOCK_W), lambda b, c: (b, c)),
        out_shape=jax.ShapeDtypeStruct((n, D_RNN), BF16),
        scratch_shapes=[pltpu.VMEM((seq, BLOCK_W), F32) for _ in range(4)],
        compiler_params=_cparams(("arbitrary", "arbitrary")),
        name="rglru",
    )(z, z, conv_w, conv_b, w4, b4, lam4)


KV_CHUNK = 512


def _qkv_body(cq_ref, ckv_ref, zk_ref, ct_ref, st_ref, qg_ref, kvg_ref,
              wqn_ref, wqr_ref, wk_ref, wvt_ref,
              qn_ref, qr_ref, kn_ref, kr_ref, vt_ref):
    scale = math.log2(math.e) / math.sqrt(QK_HEAD)
    ct = ct_ref[...]
    st = st_ref[...]
    cqn = _rms(cq_ref[...].astype(F32), qg_ref[...]).astype(BF16)
    ckvn = _rms(ckv_ref[...].astype(F32), kvg_ref[...]).astype(BF16)
    qn = jnp.dot(cqn, wqn_ref[...], preferred_element_type=F32)
    qn_ref[...] = (qn * scale).astype(BF16)
    qp = jnp.dot(cqn, wqr_ref[...], preferred_element_type=F32)
    cts = ct * scale
    sts = st * scale
    for h in range(N_HEADS):
        p = qp[:, h * LANES:(h + 1) * LANES]
        qr_ref[:, h * LANES:(h + 1) * LANES] = (p * cts + pltpu.roll(p, LANES // 2, axis=1) * sts).astype(BF16)
    kn_ref[...] = jnp.dot(ckvn, wk_ref[...], preferred_element_type=F32).astype(BF16)
    zk = zk_ref[...].astype(F32)
    kr_ref[...] = (zk * ct + pltpu.roll(zk, LANES // 2, axis=1) * st).astype(BF16)
    vt = lax.dot_general(wvt_ref[...], ckvn, (((1,), (1,)), ((), ())), preferred_element_type=F32)
    vt_ref[0] = vt.astype(BF16)


def _qkv(z, zk, ctab, stab, qg, kvg, wqn, wqr, wk, wvt, seq):
    n = z.shape[0]
    t = _tile(seq, KV_CHUNK)
    spt = seq // t
    cq_blk = Z_CQ // Q_LORA
    ckv_blk = Z_CKV // KV_LORA
    hd = N_HEADS * LANES
    const = lambda i: (0, 0)
    return pl.pallas_call(
        _qkv_body,
        grid=(n // t,),
        in_specs=[
            pl.BlockSpec((t, Q_LORA), lambda i: (i, cq_blk)),
            pl.BlockSpec((t, KV_LORA), lambda i: (i, ckv_blk)),
            pl.BlockSpec((t, LANES), lambda i: (i, 0)),
            pl.BlockSpec((t, LANES), lambda i: (i % spt, 0)),
            pl.BlockSpec((t, LANES), lambda i: (i % spt, 0)),
            pl.BlockSpec((1, Q_LORA), const),
            pl.BlockSpec((1, KV_LORA), const),
            pl.BlockSpec((Q_LORA, hd), const),
            pl.BlockSpec((Q_LORA, hd), const),
            pl.BlockSpec((KV_LORA, hd), const),
            pl.BlockSpec((hd, KV_LORA), const),
        ],
        out_specs=[
            pl.BlockSpec((t, hd), lambda i: (i, 0)),
            pl.BlockSpec((t, hd), lambda i: (i, 0)),
            pl.BlockSpec((t, hd), lambda i: (i, 0)),
            pl.BlockSpec((t, LANES), lambda i: (i, 0)),
            pl.BlockSpec((1, hd, t), lambda i: (i, 0, 0)),
        ],
        out_shape=[
            jax.ShapeDtypeStruct((n, hd), BF16),
            jax.ShapeDtypeStruct((n, hd), BF16),
            jax.ShapeDtypeStruct((n, hd), BF16),
            jax.ShapeDtypeStruct((n, LANES), BF16),
            jax.ShapeDtypeStruct((n // t, hd, t), BF16),
        ],
        compiler_params=_cparams(("arbitrary",)),
        name="qkv",
    )(z, z, zk, ctab, stab, qg, kvg, wqn, wqr, wk, wvt)


ATTN_CHAINS = 4
ONES_PAD = 16


def _attn_body(qn_ref, qr_ref, kn_ref, kr_ref, vt_ref, o_ref, *, nk, kc):
    tq = qn_ref.shape[0]
    th = tq // ATTN_CHAINS
    qs = [jnp.concatenate([qn_ref[h * th:(h + 1) * th, :], qr_ref[h * th:(h + 1) * th, :]], axis=1)
          for h in range(ATTN_CHAINS)]

    def keys(c):
        return jnp.concatenate([kn_ref[c * kc:(c + 1) * kc, :], kr_ref[c * kc:(c + 1) * kc, :]], axis=1)

    def scores(k, h):
        return lax.dot_general(k, qs[h], (((1,), (1,)), ((), ())), preferred_element_type=F32)

    ones_rows = jnp.where(lax.broadcasted_iota(I32, (ONES_PAD, kc), 0) == 0, 1.0, 0.0).astype(BF16)
    m = [jnp.full((1, th), NEG_BIG, F32) for _ in range(ATTN_CHAINS)]
    acc = [jnp.zeros((V_HEAD + ONES_PAD, th), F32) for _ in range(ATTN_CHAINS)]
    k_first = keys(0)
    s_next = [scores(k_first, h) for h in range(ATTN_CHAINS)]
    for c in range(nk):
        s_cur = s_next
        if c + 1 < nk:
            k_ahead = keys(c + 1)
            s_next = [scores(k_ahead, h) for h in range(ATTN_CHAINS)]
        v_ext = jnp.concatenate([vt_ref[c], ones_rows], axis=0)
        for h in range(ATTN_CHAINS):
            s = s_cur[h]
            m_new = jnp.maximum(m[h], jnp.max(s, axis=0, keepdims=True))
            alpha = jnp.exp2(m[h] - m_new)
            p = jnp.exp2((s - m_new).astype(BF16))
            acc[h] = alpha * acc[h] + jnp.dot(v_ext, p, preferred_element_type=F32)
            m[h] = m_new
    for h in range(ATTN_CHAINS):
        o = acc[h][:V_HEAD] * (1.0 / acc[h][V_HEAD:V_HEAD + 1])
        o_ref[h * th:(h + 1) * th, :] = o.T.astype(BF16)


def _attention(qn, qr, kn, kr, vt, batch, seq):
    n = batch * seq
    tq = _tile(seq, 2048)
    kc = _tile(seq, KV_CHUNK)
    nq = seq // tq
    nk = seq // kc
    return pl.pallas_call(
        functools.partial(_attn_body, nk=nk, kc=kc),
        grid=(batch, N_HEADS, nq),
        in_specs=[
            pl.BlockSpec((tq, LANES), lambda b, h, i: (b * nq + i, h)),
            pl.BlockSpec((tq, LANES), lambda b, h, i: (b * nq + i, h)),
            pl.BlockSpec((seq, LANES), lambda b, h, i: (b, h)),
            pl.BlockSpec((seq, LANES), lambda b, h, i: (b, 0)),
            pl.BlockSpec((nk, V_HEAD, kc), lambda b, h, i: (b, h, 0)),
        ],
        out_specs=pl.BlockSpec((tq, V_HEAD), lambda b, h, i: (b * nq + i, h)),
        out_shape=jax.ShapeDtypeStruct((n, N_HEADS * V_HEAD), BF16),
        compiler_params=_cparams(("arbitrary", "arbitrary", "arbitrary")),
        name="attn",
    )(qn, qr, kn, kr, vt)


def _outproj_body(x_ref, hg_ref, at_ref, gla_ref, glb_ref, wa_ref, wb_ref, wo_ref, g_ref, wr_ref, br_ref,
                  x1_ref, idx_ref, gate_ref, rank_ref, cnt_ref, r0_ref, carry_ref):
    t = x_ref.shape[0]

    @pl.when(pl.program_id(0) == 0)
    def _():
        carry_ref[...] = jnp.zeros(carry_ref.shape, F32)

    ya = jnp.dot(hg_ref[...], wa_ref[...], preferred_element_type=F32)
    yb = jnp.dot(at_ref[...], wb_ref[...], preferred_element_type=F32)
    m = _sigmoid(gla_ref[...].astype(F32)) * ya + _sigmoid(glb_ref[...].astype(F32)) * yb
    x1 = x_ref[...] + jnp.dot(m.astype(BF16), wo_ref[...], preferred_element_type=F32)
    x1_ref[...] = x1
    un = _rms(x1, g_ref[...])
    un_hi = un.astype(BF16)
    un_lo = (un - un_hi.astype(F32)).astype(BF16)
    w_hi = wr_ref[0]
    logits = (jnp.dot(un_hi, w_hi, preferred_element_type=F32)
              + jnp.dot(un_lo, w_hi, preferred_element_type=F32)
              + jnp.dot(un_hi, wr_ref[1], preferred_element_type=F32)).T + br_ref[...]
    e_iota = lax.broadcasted_iota(I32, logits.shape, 0)
    vals, idxs, sels = [], [], []
    cur = logits
    for _ in range(TOP_K):
        mx = jnp.max(cur, axis=0, keepdims=True)
        idx = jnp.min(jnp.where(cur == mx, e_iota, N_EXPERTS), axis=0, keepdims=True)
        sel = e_iota == idx
        vals.append(mx)
        idxs.append(idx)
        sels.append(sel)
        cur = jnp.where(sel, -jnp.inf, cur)
    ex = [jnp.exp(v - vals[0]) for v in vals]
    inv = 1.0 / (ex[0] + ex[1] + ex[2] + ex[3])
    onehot = jnp.where(sels[0] | sels[1] | sels[2] | sels[3], 1.0, 0.0)
    row = lax.broadcasted_iota(I32, (t, t), 0)
    col = lax.broadcasted_iota(I32, (t, t), 1)
    upper = jnp.where(row < col, 1.0, 0.0).astype(BF16)
    prefix = jnp.dot(onehot.astype(BF16), upper, preferred_element_type=F32) + carry_ref[:, 0:1]
    for k in range(TOP_K):
        idx_ref[k:k + 1, :] = idxs[k]
        gate_ref[k:k + 1, :] = ex[k] * inv
        rank_ref[k:k + 1, :] = jnp.sum(jnp.where(sels[k], prefix, 0.0), axis=0, keepdims=True).astype(I32)
    gate_ref[TOP_K:, :] = jnp.zeros((gate_ref.shape[0] - TOP_K, t), F32)
    r0_ref[0] = carry_ref[...]
    carry_ref[...] = carry_ref[...] + jnp.sum(onehot, axis=1, keepdims=True)
    cnt_ref[...] = carry_ref[...]


def _outproj(x2d, hg, attn, z, wa, wb, wo, g, wr2, br):
    n = x2d.shape[0]
    t = _tile(n, TOKEN_TILE)
    const = lambda i: (0, 0)
    once = pl.Buffered(1)
    return pl.pallas_call(
        _outproj_body,
        grid=(n // t,),
        in_specs=[
            pl.BlockSpec((t, D_MODEL), lambda i: (i, 0)),
            pl.BlockSpec((t, D_RNN), lambda i: (i, 0)),
            pl.BlockSpec((t, D_MODEL), lambda i: (i, 0)),
            pl.BlockSpec((t, D_MODEL), lambda i: (i, Z_GLA // D_MODEL)),
            pl.BlockSpec((t, D_MODEL), lambda i: (i, Z_GLB // D_MODEL)),
            pl.BlockSpec((D_RNN, D_MODEL), const, pipeline_mode=once),
            pl.BlockSpec((D_MODEL, D_MODEL), const, pipeline_mode=once),
            pl.BlockSpec((D_MODEL, D_MODEL), const, pipeline_mode=once),
            pl.BlockSpec((1, D_MODEL), const),
            pl.BlockSpec((2, D_MODEL, N_EXPERTS), lambda i: (0, 0, 0)),
            pl.BlockSpec((N_EXPERTS, 1), const),
        ],
        out_specs=[
            pl.BlockSpec((t, D_MODEL), lambda i: (i, 0)),
            pl.BlockSpec((TOP_K, t), lambda i: (0, i)),
            pl.BlockSpec((8, t), lambda i: (0, i)),
            pl.BlockSpec((TOP_K, t), lambda i: (0, i)),
            pl.BlockSpec((N_EXPERTS, LANES), const),
            pl.BlockSpec((1, N_EXPERTS, LANES), lambda i: (i, 0, 0)),
        ],
        out_shape=[
            jax.ShapeDtypeStruct((n, D_MODEL), F32),
            jax.ShapeDtypeStruct((TOP_K, n), I32),
            jax.ShapeDtypeStruct((8, n), F32),
            jax.ShapeDtypeStruct((TOP_K, n), I32),
            jax.ShapeDtypeStruct((N_EXPERTS, LANES), F32),
            jax.ShapeDtypeStruct((n // t, N_EXPERTS, LANES), F32),
        ],
        scratch_shapes=[pltpu.VMEM((N_EXPERTS, LANES), F32)],
        compiler_params=_cparams(("arbitrary",)),
        name="outproj",
    )(x2d, hg, attn, z, z, wa, wb, wo, g, wr2, br)


ZERO_ROWS = 256


def _dispatch_body(fs_ref, fe_ref, nu_ref, dest_ref, x1_ref, g_ref, xs_ref, un_ref, zero_ref, sem, zsem,
                   *, nsteps, nblocks):
    t = x1_ref.shape[0]
    i = pl.program_id(0)
    slot = i % 2

    def wait_slot(s):
        for _ in range(TOP_K):
            pltpu.make_async_copy(un_ref.at[s], xs_ref.at[pl.ds(0, t), :], sem.at[s]).wait()

    @pl.when(i >= 2)
    def _():
        wait_slot(slot)

    un_ref[slot] = _rms(x1_ref[...], g_ref[...])

    def issue(r, carry):
        for k in range(TOP_K):
            pltpu.make_async_copy(un_ref.at[slot, pl.ds(r, 1), :],
                                  xs_ref.at[pl.ds(dest_ref[k, r], 1), :], sem.at[slot]).start()
        return carry

    lax.fori_loop(0, t, issue, 0, unroll=8)

    @pl.when(i == 0)
    def _():
        zero_ref[...] = jnp.zeros(zero_ref.shape, F32)

        def fill_expert(e, carry):
            def start(r, c):
                pltpu.make_async_copy(zero_ref.at[pl.ds(0, 1), :], xs_ref.at[pl.ds(r, 1), :], zsem).start()
                return c

            def wait(r, c):
                pltpu.make_async_copy(zero_ref.at[pl.ds(0, 1), :], xs_ref.at[pl.ds(0, 1), :], zsem).wait()
                return c

            lax.fori_loop(fs_ref[e], fe_ref[e], start, 0)
            lax.fori_loop(fs_ref[e], fe_ref[e], wait, 0)
            return carry

        lax.fori_loop(0, N_EXPERTS, fill_expert, 0)

        def fill_block(b, carry):
            for part in range(MOE_ROWS // ZERO_ROWS):
                r0 = pl.multiple_of(b * MOE_ROWS + part * ZERO_ROWS, ZERO_ROWS)
                cp = pltpu.make_async_copy(zero_ref, xs_ref.at[pl.ds(r0, ZERO_ROWS), :], zsem)
                cp.start()
                cp.wait()
            return carry

        lax.fori_loop(nu_ref[0], nblocks, fill_block, 0)

    @pl.when(i == nsteps - 1)
    def _():
        wait_slot(slot)
        if nsteps >= 2:
            wait_slot(1 - slot)


def _dispatch(fill_start, fill_end, n_used, dest, x1, g, rows):
    n = x1.shape[0]
    t = _tile(n, 256)
    nsteps = n // t
    grid_spec = pltpu.PrefetchScalarGridSpec(
        num_scalar_prefetch=3,
        grid=(nsteps,),
        in_specs=[
            pl.BlockSpec((TOP_K, t), lambda i, fs, fe, nu: (0, i), memory_space=pltpu.SMEM),
            pl.BlockSpec((t, D_MODEL), lambda i, fs, fe, nu: (i, 0)),
            pl.BlockSpec((1, D_MODEL), lambda i, fs, fe, nu: (0, 0)),
        ],
        out_specs=pl.BlockSpec(memory_space=pl.ANY),
        scratch_shapes=[pltpu.VMEM((2, t, D_MODEL), F32), pltpu.VMEM((ZERO_ROWS, D_MODEL), F32),
                        pltpu.SemaphoreType.DMA((2,)), pltpu.SemaphoreType.DMA(())],
    )
    return pl.pallas_call(
        functools.partial(_dispatch_body, nsteps=nsteps, nblocks=rows // MOE_ROWS),
        grid_spec=grid_spec,
        out_shape=jax.ShapeDtypeStruct((rows, D_MODEL), F32),
        compiler_params=_cparams(("arbitrary",)),
        name="dispatch",
    )(fill_start, fill_end, n_used, dest, x1, g)


MOE_ROWS = 512
MOE_FF = 1024


def _experts_body(be_ref, nu_ref, xs_ref, wg_ref, wu_ref, bg_ref, bu_ref, wd_ref, bd_ref,
                  ys_ref, xb_ref, acc_ref):
    i = pl.program_id(0)
    f = pl.program_id(1)
    nf = D_FF // MOE_FF

    @pl.when(jnp.logical_and(i >= nu_ref[0], f == 0))
    def _():
        ys_ref[...] = jnp.zeros(ys_ref.shape, BF16)

    @pl.when(i < nu_ref[0])
    def _():
        @pl.when(f == 0)
        def _():
            xb_ref[...] = xs_ref[...].astype(BF16)

        xb = xb_ref[...]
        nt = (((1,), (1,)), ((), ()))
        g = lax.dot_general(xb, wg_ref[0], nt, preferred_element_type=F32) + bg_ref[0]
        u = lax.dot_general(xb, wu_ref[0], nt, preferred_element_type=F32) + bu_ref[0]
        g = jnp.minimum(g, SWIGLU_LIMIT)
        u = jnp.clip(u, -SWIGLU_LIMIT, SWIGLU_LIMIT)
        h = (g * _sigmoid(SWIGLU_ALPHA * g) * (u + 1.0)).astype(BF16)
        y = jnp.dot(h, wd_ref[0], preferred_element_type=F32)

        @pl.when(f == 0)
        def _():
            acc_ref[...] = y + bd_ref[0]

        @pl.when(jnp.logical_and(f > 0, f < nf - 1))
        def _():
            acc_ref[...] = acc_ref[...] + y

        @pl.when(f == nf - 1)
        def _():
            ys_ref[...] = (acc_ref[...] + y).astype(BF16)


def _experts(block_e, n_used, xs, wg_t, wu_t, bg, bu, wd, bd):
    rows = xs.shape[0]
    tm = MOE_ROWS
    nb = rows // tm
    nf = D_FF // MOE_FF

    def blk(i, nu):
        return jnp.minimum(i, nu[0] - 1)

    def fidx(i, f, nu):
        return jnp.where(i < nu[0], f, nf - 1)

    grid_spec = pltpu.PrefetchScalarGridSpec(
        num_scalar_prefetch=2,
        grid=(nb, nf),
        in_specs=[
            pl.BlockSpec((tm, D_MODEL), lambda i, f, be, nu: (blk(i, nu), 0)),
            pl.BlockSpec((1, MOE_FF, D_MODEL), lambda i, f, be, nu: (be[blk(i, nu)], fidx(i, f, nu), 0)),
            pl.BlockSpec((1, MOE_FF, D_MODEL), lambda i, f, be, nu: (be[blk(i, nu)], fidx(i, f, nu), 0)),
            pl.BlockSpec((1, 1, MOE_FF), lambda i, f, be, nu: (be[blk(i, nu)], 0, fidx(i, f, nu))),
            pl.BlockSpec((1, 1, MOE_FF), lambda i, f, be, nu: (be[blk(i, nu)], 0, fidx(i, f, nu))),
            pl.BlockSpec((1, MOE_FF, D_MODEL), lambda i, f, be, nu: (be[blk(i, nu)], fidx(i, f, nu), 0)),
            pl.BlockSpec((1, 1, D_MODEL), lambda i, f, be, nu: (be[blk(i, nu)], 0, 0)),
        ],
        out_specs=pl.BlockSpec((tm, D_MODEL), lambda i, f, be, nu: (i, 0)),
        scratch_shapes=[pltpu.VMEM((tm, D_MODEL), BF16), pltpu.VMEM((tm, D_MODEL), F32)],
    )
    assert nf >= 2
    return pl.pallas_call(
        _experts_body,
        grid_spec=grid_spec,
        out_shape=jax.ShapeDtypeStruct((rows, D_MODEL), BF16),
        compiler_params=_cparams(("arbitrary", "arbitrary")),
        name="experts",
    )(block_e, n_used, xs, wg_t, wu_t, bg, bu, wd, bd)


TOKEN_TILE = 256
WIN_ROWS = 64
WIN_ALIGN = 16


def _combine_body(base_ref, npass_ref, x1_ref, gate_ref, pos_ref, g_ref, ys_ref, o_ref,
                  win_ref, acc_ref, sem, *, nsteps, last_start):
    t = x1_ref.shape[0]
    kdim = N_EXPERTS * WIN_ROWS
    i = pl.program_id(0)
    slot = i % 2

    def window_copies(tile, p, dst_slot, dst_sem):
        for e in range(N_EXPERTS):
            start = jnp.minimum(base_ref[tile * N_EXPERTS + e] + p * WIN_ROWS, last_start)
            start = pl.multiple_of(start, WIN_ALIGN)
            yield pltpu.make_async_copy(ys_ref.at[pl.ds(start, WIN_ROWS), :],
                                        win_ref.at[dst_slot, pl.ds(e * WIN_ROWS, WIN_ROWS), :], dst_sem)

    def wait_windows(dst_slot, dst_sem):
        pltpu.make_async_copy(ys_ref.at[pl.ds(0, kdim), :], win_ref.at[dst_slot], dst_sem).wait()

    @pl.when(i == 0)
    def _():
        for cp in window_copies(0, 0, 0, sem.at[0]):
            cp.start()

    @pl.when(i + 1 < nsteps)
    def _():
        for cp in window_copies(i + 1, 0, 1 - slot, sem.at[1 - slot]):
            cp.start()

    gcol = gate_ref[...].T
    pcol = pos_ref[...].T
    lane = lax.broadcasted_iota(I32, (t, kdim), 1).astype(F32)

    def weights(p):
        w = jnp.zeros((t, kdim), F32)
        for k in range(TOP_K):
            pos_k = jnp.where(pcol[:, TOP_K + k:TOP_K + k + 1] == p, pcol[:, k:k + 1], -1.0)
            w = jnp.where(lane == pos_k, gcol[:, k:k + 1], w)
        return w.astype(BF16)

    wait_windows(slot, sem.at[slot])
    acc_ref[...] = x1_ref[...] + jnp.dot(weights(0.0), win_ref[slot], preferred_element_type=F32)

    def extra_pass(p, carry):
        for cp in window_copies(i, p, slot, sem.at[2]):
            cp.start()
        wait_windows(slot, sem.at[2])
        acc_ref[...] = acc_ref[...] + jnp.dot(weights(p.astype(F32)), win_ref[slot], preferred_element_type=F32)
        return carry

    lax.fori_loop(1, npass_ref[i], extra_pass, 0)
    o_ref[...] = _rms(acc_ref[...], g_ref[...])


def _combine(base_al, npass, x1, gate, posf, g, ys):
    n = x1.shape[0]
    rows = ys.shape[0]
    t = _tile(n, TOKEN_TILE)
    nsteps = n // t
    kdim = N_EXPERTS * WIN_ROWS
    grid_spec = pltpu.PrefetchScalarGridSpec(
        num_scalar_prefetch=2,
        grid=(nsteps,),
        in_specs=[
            pl.BlockSpec((t, D_MODEL), lambda i, b, q: (i, 0)),
            pl.BlockSpec((8, t), lambda i, b, q: (0, i)),
            pl.BlockSpec((8, t), lambda i, b, q: (0, i)),
            pl.BlockSpec((1, D_MODEL), lambda i, b, q: (0, 0)),
            pl.BlockSpec(memory_space=pl.ANY),
        ],
        out_specs=pl.BlockSpec((t, D_MODEL), lambda i, b, q: (i, 0)),
        scratch_shapes=[pltpu.VMEM((2, kdim, D_MODEL), BF16), pltpu.VMEM((t, D_MODEL), F32),
                        pltpu.SemaphoreType.DMA((3,))],
    )
    return pl.pallas_call(
        functools.partial(_combine_body, nsteps=nsteps, last_start=rows - WIN_ROWS),
        grid_spec=grid_spec,
        out_shape=jax.ShapeDtypeStruct((n, D_MODEL), F32),
        compiler_params=_cparams(("arbitrary",)),
        name="combine",
    )(base_al, npass, x1, gate, posf, g, ys)


PREP_K = 512


def _prep_gate_up_body(w_ref, g_ref, u_ref, t_ref):
    wt = w_ref[0].T
    for c in range(PREP_K // LANES):
        cols = slice(c * LANES, (c + 1) * LANES)
        t_ref[c] = wt[:, cols]
        g_ref[0, :, cols] = t_ref[c, pl.ds(0, D_FF, stride=2), :].astype(BF16)
        u_ref[0, :, cols] = t_ref[c, pl.ds(1, D_FF, stride=2), :].astype(BF16)


def _prep_gate_up(w_gate_up):
    out = jax.ShapeDtypeStruct((N_EXPERTS, D_FF, D_MODEL), BF16)
    return pl.pallas_call(
        _prep_gate_up_body,
        grid=(N_EXPERTS, D_MODEL // PREP_K),
        in_specs=[pl.BlockSpec((1, PREP_K, 2 * D_FF), lambda e, k: (e, k, 0))],
        out_specs=[pl.BlockSpec((1, D_FF, PREP_K), lambda e, k: (e, 0, k)),
                   pl.BlockSpec((1, D_FF, PREP_K), lambda e, k: (e, 0, k))],
        out_shape=[out, out],
        scratch_shapes=[pltpu.VMEM((PREP_K // LANES, 2 * D_FF, LANES), F32)],
        compiler_params=_cparams(("arbitrary", "arbitrary")),
        name="prep_gate_up",
    )(w_gate_up)


def _cast_body(w_ref, o_ref):
    o_ref[...] = w_ref[...].astype(BF16)


def _prep_cast(w):
    e, r, c = w.shape
    tr = _tile(r, 1024)
    return pl.pallas_call(
        _cast_body,
        grid=(e, r // tr),
        in_specs=[pl.BlockSpec((1, tr, c), lambda i, j: (i, j, 0))],
        out_specs=pl.BlockSpec((1, tr, c), lambda i, j: (i, j, 0)),
        out_shape=jax.ShapeDtypeStruct(w.shape, BF16),
        compiler_params=_cparams(("arbitrary", "arbitrary")),
        name="prep_cast",
    )(w)


def _rot_cols(w):
    half = QK_ROPE // 2
    return jnp.concatenate([-w[..., half:], w[..., :half]], axis=-1)


def _prep_weights(norm_mix_g, w_in, conv_w, conv_b, rg_w_a, rg_b_a, rg_w_x, rg_b_x, rg_lambda,
                  w_proj_a, q_norm_g, w_uq, kv_norm_g, w_uk, w_uv, w_proj_b, w_out,
                  norm_ffn_g, w_router, b_router, w_gate_up, b_gate_up, w_down, b_down, norm_final_g):
    p = {}
    wi = w_in[0]
    o_xa, o_ga, o_cq, o_ckv, o_kr, o_gla, o_glb = 0, 1024, 2048, 2560, 3072, 3136, 5184
    p["w_main"] = jnp.concatenate([
        wi[:, o_gla:o_gla + D_MODEL], wi[:, o_glb:o_glb + D_MODEL],
        wi[:, o_xa:o_xa + D_RNN], wi[:, o_ga:o_ga + D_RNN],
        wi[:, o_cq:o_cq + Q_LORA], wi[:, o_ckv:o_ckv + KV_LORA]], axis=1).astype(BF16)
    wkr = wi[:, o_kr:o_kr + QK_ROPE]
    p["w_k2"] = jnp.concatenate([wkr, _rot_cols(wkr)], axis=1).astype(BF16)
    p["g_mix"] = norm_mix_g[0].reshape(1, D_MODEL)
    p["conv_w"] = conv_w[0]
    p["conv_b"] = conv_b[0].reshape(1, D_RNN)
    wa, wx = rg_w_a[0], rg_w_x[0]
    p["w4"] = jnp.concatenate([wa[0], wx[0], wa[1], wx[1]], axis=-1).astype(BF16)
    ba = rg_b_a[0].reshape(2, RNN_BLOCKS, 1, BLOCK_W)
    bx = rg_b_x[0].reshape(2, RNN_BLOCKS, 1, BLOCK_W)
    p["b4"] = jnp.concatenate([ba[0], bx[0], ba[1], bx[1]], axis=-1)
    lam = rg_lambda[0].reshape(2, RNN_BLOCKS, 1, BLOCK_W)
    p["lam4"] = jnp.concatenate([lam[0], lam[1]], axis=-1)
    p["w_a"] = w_proj_a[0].astype(BF16)
    p["q_g"] = q_norm_g[0].reshape(1, Q_LORA)
    p["kv_g"] = kv_norm_g[0].reshape(1, KV_LORA)
    wq = w_uq[0].reshape(Q_LORA, N_HEADS, QK_HEAD)
    p["wq_n"] = wq[:, :, :QK_NOPE].reshape(Q_LORA, N_HEADS * QK_NOPE).astype(BF16)
    wqr = wq[:, :, QK_NOPE:]
    p["wq_r"] = jnp.concatenate([wqr, _rot_cols(wqr)], axis=-1).reshape(Q_LORA, N_HEADS * LANES).astype(BF16)
    p["w_k"] = w_uk[0].astype(BF16)
    p["w_vt"] = w_uv[0].T.astype(BF16)
    p["w_b"] = w_proj_b[0].astype(BF16)
    p["w_o"] = w_out[0].astype(BF16)
    p["g_ffn"] = norm_ffn_g[0].reshape(1, D_MODEL)
    w_r_hi = w_router[0].astype(BF16)
    w_r_lo = (w_router[0] - w_r_hi.astype(F32)).astype(BF16)
    p["w_r2"] = jnp.stack([w_r_hi, w_r_lo])
    p["b_r"] = b_router[0].reshape(N_EXPERTS, 1)
    p["w_g_t"], p["w_u_t"] = _prep_gate_up(w_gate_up[0])
    bgu = b_gate_up[0]
    p["b_g"] = bgu[:, 0::2].reshape(N_EXPERTS, 1, D_FF)
    p["b_u"] = bgu[:, 1::2].reshape(N_EXPERTS, 1, D_FF)
    p["w_d"] = _prep_cast(w_down[0])
    p["b_d"] = b_down[0].reshape(N_EXPERTS, 1, D_MODEL)
    p["g_fin"] = norm_final_g.reshape(1, D_MODEL)
    return p


def _rope_tables(seq):
    half = QK_ROPE // 2
    freqs = ROPE_THETA ** (-jnp.arange(half, dtype=F32) / half)
    ang = jnp.arange(seq, dtype=F32)[:, None] * freqs[None, :]
    zero = jnp.zeros((seq, LANES - QK_ROPE), F32)
    ctab = jnp.concatenate([jnp.cos(ang), jnp.cos(ang), zero], axis=1)
    stab = jnp.concatenate([jnp.sin(ang), jnp.sin(ang), zero], axis=1)
    return ctab, stab


def _trunk(x, p):
    batch, seq, _ = x.shape
    n = batch * seq
    x2d = x.reshape(n, D_MODEL)
    z, zk = _in_proj(x2d, p["g_mix"], p["w_main"], p["w_k2"])
    hg = _rglru(z, p["conv_w"], p["conv_b"], p["w4"], p["b4"], p["lam4"], batch, seq)
    ctab, stab = _rope_tables(seq)
    qn, qr, kn, kr, vt = _qkv(z, zk, ctab, stab, p["q_g"], p["kv_g"],
                              p["wq_n"], p["wq_r"], p["w_k"], p["w_vt"], seq)
    attn = _attention(qn, qr, kn, kr, vt, batch, seq)
    x1, idx, gate, rank, cnt, r0 = _outproj(x2d, hg, attn, z, p["w_a"], p["w_b"], p["w_o"],
                                        p["g_ffn"], p["w_r2"], p["b_r"])

    tm = MOE_ROWS
    counts = cnt[:, 0].astype(I32)
    padded = (counts + tm - 1) // tm * tm
    pad_end = jnp.cumsum(padded)
    pad_start = pad_end - padded
    experts = jnp.arange(N_EXPERTS, dtype=I32)
    dest = jnp.sum(jnp.where(idx[:, :, None] == experts, pad_start, 0), axis=-1) + rank
    nb = -(-(n * TOP_K) // tm) + N_EXPERTS
    rows = nb * tm
    blk_start = jnp.arange(nb, dtype=I32) * tm
    block_e = jnp.minimum(jnp.sum((pad_end[None, :] <= blk_start[:, None]).astype(I32), axis=1), N_EXPERTS - 1)
    n_used = (pad_end[-1:] // tm).astype(I32)

    xs = _dispatch(pad_start + counts, pad_end, n_used, dest, x1, p["g_ffn"], rows)
    ys = _experts(block_e, n_used, xs, p["w_g_t"], p["w_u_t"], p["b_g"], p["b_u"], p["w_d"], p["b_d"])

    t = _tile(n, TOKEN_TILE)
    last_start = rows - WIN_ROWS
    run_start = pad_start[None, :] + r0[:, :, 0].astype(I32)
    base_al = jnp.minimum(run_start // WIN_ALIGN * WIN_ALIGN, last_start)
    tok_base = jnp.repeat(base_al, t, axis=0)
    base_k = jnp.sum(jnp.where(idx[:, :, None] == experts, tok_base[None], 0), axis=-1)
    pass_k = (dest - base_k) // WIN_ROWS
    start_k = jnp.minimum(base_k + pass_k * WIN_ROWS, last_start)
    pos_k = idx * WIN_ROWS + (dest - start_k)
    posf = jnp.concatenate([pos_k, pass_k], axis=0).astype(F32)
    npass = (jnp.max(pass_k.reshape(TOP_K, n // t, t), axis=(0, 2)) + 1).astype(I32)
    out = _combine(base_al.reshape(-1), npass, x1, gate, posf, p["g_fin"], ys)
    return out.reshape(batch, seq, D_MODEL)


def kernel(x_prompt, x_sample, norm_mix_g, w_in, conv_w, conv_b, rg_w_a, rg_b_a, rg_w_x, rg_b_x, rg_lambda,
           w_proj_a, q_norm_g, w_uq, kv_norm_g, w_uk, w_uv, w_proj_b, w_out, norm_ffn_g, w_router, b_router,
           w_gate_up, b_gate_up, w_down, b_down, norm_final_g):
    p = _prep_weights(norm_mix_g, w_in, conv_w, conv_b, rg_w_a, rg_b_a, rg_w_x, rg_b_x, rg_lambda,
                      w_proj_a, q_norm_g, w_uq, kv_norm_g, w_uk, w_uv, w_proj_b, w_out,
                      norm_ffn_g, w_router, b_router, w_gate_up, b_gate_up, w_down, b_down, norm_final_g)
    return (_trunk(x_prompt, p), _trunk(x_sample, p))
```

```python
import functools
import math

import jax
import jax.numpy as jnp
from jax import lax
from jax.experimental import pallas as pl
from jax.experimental.pallas import tpu as pltpu

F32 = jnp.float32
BF16 = jnp.bfloat16
I32 = jnp.int32

D_MODEL = 2048
D_RNN = 1024
RNN_BLOCKS = 8
BLOCK_W = 128
C_RG = 8.0
N_HEADS = 16
QK_NOPE = 128
QK_ROPE = 64
QK_HEAD = QK_NOPE + QK_ROPE
V_HEAD = 128
Q_LORA = 512
KV_LORA = 512
ROPE_THETA = 10000.0
N_EXPERTS = 32
TOP_K = 4
D_FF = 2048
SWIGLU_LIMIT = 7.0
SWIGLU_ALPHA = 1.702
EPS = 1e-6

LANES = 128
VMEM_LIMIT = 56 * 1024 * 1024
NEG_BIG = -1e30

Z_GLA, Z_GLB, Z_XA, Z_GA, Z_CQ, Z_CKV = 0, 2048, 4096, 5120, 6144, 6656
Z_COLS = 7168


def _cparams(sem):
    return pltpu.CompilerParams(dimension_semantics=sem, vmem_limit_bytes=VMEM_LIMIT)


def _tile(n, pref):
    t = min(n, pref)
    assert n % t == 0, (n, pref)
    return t


def _rms(x, g):
    return x * lax.rsqrt(jnp.mean(x * x, axis=-1, keepdims=True) + EPS) * g


def _in_proj_body(x_ref, g_ref, w_ref, wk_ref, z_ref, zk_ref, u_ref):
    @pl.when(pl.program_id(1) == 0)
    def _():
        u = _rms(x_ref[...], g_ref[...]).astype(BF16)
        u_ref[...] = u
        zk_ref[...] = jnp.dot(u, wk_ref[...], preferred_element_type=F32).astype(BF16)

    z_ref[...] = jnp.dot(u_ref[...], w_ref[...], preferred_element_type=F32).astype(BF16)


def _in_proj(x2d, g, w_main, w_k2):
    n = x2d.shape[0]
    tm = _tile(n, 1024)
    tn = 1024
    return pl.pallas_call(
        _in_proj_body,
        grid=(n // tm, Z_COLS // tn),
        in_specs=[
            pl.BlockSpec((tm, D_MODEL), lambda i, j: (i, 0)),
            pl.BlockSpec((1, D_MODEL), lambda i, j: (0, 0)),
            pl.BlockSpec((D_MODEL, tn), lambda i, j: (0, j)),
            pl.BlockSpec((D_MODEL, LANES), lambda i, j: (0, 0)),
        ],
        out_specs=[
            pl.BlockSpec((tm, tn), lambda i, j: (i, j)),
            pl.BlockSpec((tm, LANES), lambda i, j: (i, 0)),
        ],
        out_shape=[
            jax.ShapeDtypeStruct((n, Z_COLS), BF16),
            jax.ShapeDtypeStruct((n, LANES), BF16),
        ],
        scratch_shapes=[pltpu.VMEM((tm, D_MODEL), BF16)],
        compiler_params=_cparams(("arbitrary", "arbitrary")),
        name="in_proj",
    )(x2d, g, w_main, w_k2)


HALO = 16
SCAN_ROWS = 128


def _sigmoid(x):
    return 0.5 * jnp.tanh(0.5 * x) + 0.5


def _gelu_tanh(x):
    return 0.5 * x * (1.0 + jnp.tanh(math.sqrt(2.0 / math.pi) * (x + 0.044715 * (x * x * x))))


def _scan_chunk(a, b, reverse):
    rows = a.shape[0]
    row = lax.broadcasted_iota(I32, a.shape, 0)
    d = 1
    while d < rows:
        if d < 8:
            shift = (rows - d) if reverse else d
            a_sh = pltpu.roll(a, shift, axis=0)
            b_sh = pltpu.roll(b, shift, axis=0)
            keep = (row < rows - d) if reverse else (row >= d)
            a_sh = jnp.where(keep, a_sh, 1.0)
            b_sh = jnp.where(keep, b_sh, 0.0)
        else:
            one = jnp.ones((d, a.shape[1]), F32)
            zero = jnp.zeros((d, a.shape[1]), F32)
            if reverse:
                a_sh = jnp.concatenate([a[d:], one], axis=0)
                b_sh = jnp.concatenate([b[d:], zero], axis=0)
            else:
                a_sh = jnp.concatenate([one, a[:-d]], axis=0)
                b_sh = jnp.concatenate([zero, b[:-d]], axis=0)
        b = a * b_sh + b
        a = a * a_sh
        d *= 2
    return a, b


def _rglru_body(xa_ref, ga_ref, cw_ref, cb_ref, w4_ref, b4_ref, lam_ref, o_ref,
                a0_ref, b0_ref, a1_ref, b1_ref, *, seq, tc):
    nchunk = seq // tc
    lam = lam_ref[0]
    sp = jnp.maximum(-lam, 0.0) + jnp.log1p(jnp.exp(-jnp.abs(lam)))
    sp0 = sp[:, :BLOCK_W]
    sp1 = sp[:, BLOCK_W:]
    cw = cw_ref[...]
    cb = cb_ref[...]
    w4 = w4_ref[0]
    b4 = b4_ref[0]

    def gates(j, carry):
        r0 = pl.multiple_of(j * tc, tc)
        cur = xa_ref[pl.ds(r0, tc), :].astype(F32)
        prev_start = pl.multiple_of(jnp.maximum(r0 - HALO, 0), HALO)
        next_start = pl.multiple_of(jnp.minimum(r0 + tc, seq - HALO), HALO)
        prev = xa_ref[pl.ds(prev_start, HALO), :].astype(F32)
        nxt = xa_ref[pl.ds(next_start, HALO), :].astype(F32)
        prev = jnp.where(j > 0, prev, 0.0)
        nxt = jnp.where(j < nchunk - 1, nxt, 0.0)
        ext = jnp.concatenate([prev, cur, nxt], axis=0)
        n_ext = tc + 2 * HALO
        xm2 = pltpu.roll(ext, 2, axis=0)[HALO:HALO + tc]
        xm1 = pltpu.roll(ext, 1, axis=0)[HALO:HALO + tc]
        xp1 = pltpu.roll(ext, n_ext - 1, axis=0)[HALO:HALO + tc]
        xc = xm2 * cw[0:1] + xm1 * cw[1:2] + cur * cw[2:3] + xp1 * cw[3:4] + cb
        pre = jnp.dot(xc.astype(BF16), w4, preferred_element_type=F32) + b4
        for d, (a_ref, b_ref, spd) in enumerate(((a0_ref, b0_ref, sp0), (a1_ref, b1_ref, sp1))):
            r = _sigmoid(pre[:, (2 * d) * BLOCK_W:(2 * d + 1) * BLOCK_W])
            i = _sigmoid(pre[:, (2 * d + 1) * BLOCK_W:(2 * d + 2) * BLOCK_W])
            a = jnp.exp((-C_RG) * r * spd)
            b = jnp.sqrt(1.0 - a * a) * (i * xc)
            a_ref[pl.ds(r0, tc), :] = a
            b_ref[pl.ds(r0, tc), :] = b
        return carry

    lax.fori_loop(0, nchunk, gates, 0, unroll=2)

    nscan = seq // SCAN_ROWS

    def bwd(jj, h):
        j = nscan - 1 - jj
        r0 = pl.multiple_of(j * SCAN_ROWS, SCAN_ROWS)
        A, B = _scan_chunk(a1_ref[pl.ds(r0, SCAN_ROWS), :], b1_ref[pl.ds(r0, SCAN_ROWS), :], True)
        hh = A * h + B
        b1_ref[pl.ds(r0, SCAN_ROWS), :] = hh
        return hh[0:1, :]

    lax.fori_loop(0, nscan, bwd, jnp.zeros((1, BLOCK_W), F32))

    def fwd(j, h):
        r0 = pl.multiple_of(j * SCAN_ROWS, SCAN_ROWS)
        A, B = _scan_chunk(a0_ref[pl.ds(r0, SCAN_ROWS), :], b0_ref[pl.ds(r0, SCAN_ROWS), :], False)
        hh = A * h + B
        ga = ga_ref[pl.ds(r0, SCAN_ROWS), :].astype(F32)
        o_ref[pl.ds(r0, SCAN_ROWS), :] = ((hh + b1_ref[pl.ds(r0, SCAN_ROWS), :]) * _gelu_tanh(ga)).astype(BF16)
        return hh[SCAN_ROWS - 1:SCAN_ROWS, :]

    lax.fori_loop(0, nscan, fwd, jnp.zeros((1, BLOCK_W), F32))


def _rglru(z, conv_w, conv_b, w4, b4, lam4, batch, seq):
    n = batch * seq
    tc = _tile(seq, 256)
    xa_blk = Z_XA // BLOCK_W
    ga_blk = Z_GA // BLOCK_W
    return pl.pallas_call(
        functools.partial(_rglru_body, seq=seq, tc=tc),
        grid=(batch, RNN_BLOCKS),
        in_specs=[
            pl.BlockSpec((seq, BLOCK_W), lambda b, c: (b, xa_blk + c)),
            pl.BlockSpec((seq, BLOCK_W), lambda b, c: (b, ga_blk + c)),
            pl.BlockSpec((4, BLOCK_W), lambda b, c: (0, c)),
            pl.BlockSpec((1, BLOCK_W), lambda b, c: (0, c)),
            pl.BlockSpec((1, BLOCK_W, 4 * BLOCK_W), lambda b, c: (c, 0, 0)),
            pl.BlockSpec((1, 1, 4 * BLOCK_W), lambda b, c: (c, 0, 0)),
            pl.BlockSpec((1, 1, 2 * BLOCK_W), lambda b, c: (c, 0, 0)),
        ],
        out_specs=pl.BlockSpec((seq, BLOCK_W), lambda b, c: (b, c)),
        out_shape=jax.ShapeDtypeStruct((n, D_RNN), BF16),
        scratch_shapes=[pltpu.VMEM((seq, BLOCK_W), F32) for _ in range(4)],
        compiler_params=_cparams(("arbitrary", "arbitrary")),
        name="rglru",
    )(z, z, conv_w, conv_b, w4, b4, lam4)


KV_CHUNK = 512


def _qkv_body(cq_ref, ckv_ref, zk_ref, ct_ref, st_ref, qg_ref, kvg_ref,
              wqn_ref, wqr_ref, wk_ref, wvt_ref,
              qn_ref, qr_ref, kn_ref, kr_ref, vt_ref):
    scale = math.log2(math.e) / math.sqrt(QK_HEAD)
    ct = ct_ref[...]
    st = st_ref[...]
    cqn = _rms(cq_ref[...].astype(F32), qg_ref[...]).astype(BF16)
    ckvn = _rms(ckv_ref[...].astype(F32), kvg_ref[...]).astype(BF16)
    qn = jnp.dot(cqn, wqn_ref[...], preferred_element_type=F32)
    qn_ref[...] = (qn * scale).astype(BF16)
    qp = jnp.dot(cqn, wqr_ref[...], preferred_element_type=F32)
    cts = ct * scale
    sts = st * scale
    for h in range(N_HEADS):
        p = qp[:, h * LANES:(h + 1) * LANES]
        qr_ref[:, h * LANES:(h + 1) * LANES] = (p * cts + pltpu.roll(p, LANES // 2, axis=1) * sts).astype(BF16)
    kn_ref[...] = jnp.dot(ckvn, wk_ref[...], preferred_element_type=F32).astype(BF16)
    zk = zk_ref[...].astype(F32)
    kr_ref[...] = (zk * ct + pltpu.roll(zk, LANES // 2, axis=1) * st).astype(BF16)
    vt = lax.dot_general(wvt_ref[...], ckvn, (((1,), (1,)), ((), ())), preferred_element_type=F32)
    vt_ref[0] = vt.astype(BF16)


def _qkv(z, zk, ctab, stab, qg, kvg, wqn, wqr, wk, wvt, seq):
    n = z.shape[0]
    t = _tile(seq, KV_CHUNK)
    spt = seq // t
    cq_blk = Z_CQ // Q_LORA
    ckv_blk = Z_CKV // KV_LORA
    hd = N_HEADS * LANES
    const = lambda i: (0, 0)
    return pl.pallas_call(
        _qkv_body,
        grid=(n // t,),
        in_specs=[
            pl.BlockSpec((t, Q_LORA), lambda i: (i, cq_blk)),
            pl.BlockSpec((t, KV_LORA), lambda i: (i, ckv_blk)),
            pl.BlockSpec((t, LANES), lambda i: (i, 0)),
            pl.BlockSpec((t, LANES), lambda i: (i % spt, 0)),
            pl.BlockSpec((t, LANES), lambda i: (i % spt, 0)),
            pl.BlockSpec((1, Q_LORA), const),
            pl.BlockSpec((1, KV_LORA), const),
            pl.BlockSpec((Q_LORA, hd), const),
            pl.BlockSpec((Q_LORA, hd), const),
            pl.BlockSpec((KV_LORA, hd), const),
            pl.BlockSpec((hd, KV_LORA), const),
        ],
        out_specs=[
            pl.BlockSpec((t, hd), lambda i: (i, 0)),
            pl.BlockSpec((t, hd), lambda i: (i, 0)),
            pl.BlockSpec((t, hd), lambda i: (i, 0)),
            pl.BlockSpec((t, LANES), lambda i: (i, 0)),
            pl.BlockSpec((1, hd, t), lambda i: (i, 0, 0)),
        ],
        out_shape=[
            jax.ShapeDtypeStruct((n, hd), BF16),
            jax.ShapeDtypeStruct((n, hd), BF16),
            jax.ShapeDtypeStruct((n, hd), BF16),
            jax.ShapeDtypeStruct((n, LANES), BF16),
            jax.ShapeDtypeStruct((n // t, hd, t), BF16),
        ],
        compiler_params=_cparams(("arbitrary",)),
        name="qkv",
    )(z, z, zk, ctab, stab, qg, kvg, wqn, wqr, wk, wvt)


ATTN_CHAINS = 4
ONES_PAD = 16


def _attn_body(qn_ref, qr_ref, kn_ref, kr_ref, vt_ref, o_ref, *, nk, kc):
    tq = qn_ref.shape[0]
    th = tq // ATTN_CHAINS
    qs = [jnp.concatenate([qn_ref[h * th:(h + 1) * th, :], qr_ref[h * th:(h + 1) * th, :]], axis=1)
          for h in range(ATTN_CHAINS)]

    def keys(c):
        return jnp.concatenate([kn_ref[c * kc:(c + 1) * kc, :], kr_ref[c * kc:(c + 1) * kc, :]], axis=1)

    def scores(k, h):
        return lax.dot_general(k, qs[h], (((1,), (1,)), ((), ())), preferred_element_type=F32)

    ones_rows = jnp.where(lax.broadcasted_iota(I32, (ONES_PAD, kc), 0) == 0, 1.0, 0.0).astype(BF16)
    m = [jnp.full((1, th), NEG_BIG, F32) for _ in range(ATTN_CHAINS)]
    acc = [jnp.zeros((V_HEAD + ONES_PAD, th), F32) for _ in range(ATTN_CHAINS)]
    k_first = keys(0)
    s_next = [scores(k_first, h) for h in range(ATTN_CHAINS)]
    for c in range(nk):
        s_cur = s_next
        if c + 1 < nk:
            k_ahead = keys(c + 1)
            s_next = [scores(k_ahead, h) for h in range(ATTN_CHAINS)]
        v_ext = jnp.concatenate([vt_ref[c], ones_rows], axis=0)
        for h in range(ATTN_CHAINS):
            s = s_cur[h]
            m_new = jnp.maximum(m[h], jnp.max(s, axis=0, keepdims=True))
            alpha = jnp.exp2(m[h] - m_new)
            p = jnp.exp2((s - m_new).astype(BF16))
            acc[h] = alpha * acc[h] + jnp.dot(v_ext, p, preferred_element_type=F32)
            m[h] = m_new
    for h in range(ATTN_CHAINS):
        o = acc[h][:V_HEAD] * (1.0 / acc[h][V_HEAD:V_HEAD + 1])
        o_ref[h * th:(h + 1) * th, :] = o.T.astype(BF16)


def _attention(qn, qr, kn, kr, vt, batch, seq):
    n = batch * seq
    tq = _tile(seq, 2048)
    kc = _tile(seq, KV_CHUNK)
    nq = seq // tq
    nk = seq // kc
    return pl.pallas_call(
        functools.partial(_attn_body, nk=nk, kc=kc),
        grid=(batch, N_HEADS, nq),
        in_specs=[
            pl.BlockSpec((tq, LANES), lambda b, h, i: (b * nq + i, h)),
            pl.BlockSpec((tq, LANES), lambda b, h, i: (b * nq + i, h)),
            pl.BlockSpec((seq, LANES), lambda b, h, i: (b, h)),
            pl.BlockSpec((seq, LANES), lambda b, h, i: (b, 0)),
            pl.BlockSpec((nk, V_HEAD, kc), lambda b, h, i: (b, h, 0)),
        ],
        out_specs=pl.BlockSpec((tq, V_HEAD), lambda b, h, i: (b * nq + i, h)),
        out_shape=jax.ShapeDtypeStruct((n, N_HEADS * V_HEAD), BF16),
        compiler_params=_cparams(("arbitrary", "arbitrary", "arbitrary")),
        name="attn",
    )(qn, qr, kn, kr, vt)


def _outproj_body(x_ref, hg_ref, at_ref, gla_ref, glb_ref, wa_ref, wb_ref, wo_ref, g_ref, wr_ref, br_ref,
                  x1_ref, idx_ref, gate_ref, rank_ref, cnt_ref, r0_ref, carry_ref):
    t = x_ref.shape[0]

    @pl.when(pl.program_id(0) == 0)
    def _():
        carry_ref[...] = jnp.zeros(carry_ref.shape, F32)

    ya = jnp.dot(hg_ref[...], wa_ref[...], preferred_element_type=F32)
    yb = jnp.dot(at_ref[...], wb_ref[...], preferred_element_type=F32)
    m = _sigmoid(gla_ref[...].astype(F32)) * ya + _sigmoid(glb_ref[...].astype(F32)) * yb
    x1 = x_ref[...] + jnp.dot(m.astype(BF16), wo_ref[...], preferred_element_type=F32)
    x1_ref[...] = x1
    un = _rms(x1, g_ref[...])
    un_hi = un.astype(BF16)
    un_lo = (un - un_hi.astype(F32)).astype(BF16)
    w_hi = wr_ref[0]
    logits = (jnp.dot(un_hi, w_hi, preferred_element_type=F32)
              + jnp.dot(un_lo, w_hi, preferred_element_type=F32)
              + jnp.dot(un_hi, wr_ref[1], preferred_element_type=F32)).T + br_ref[...]
    e_iota = lax.broadcasted_iota(I32, logits.shape, 0)
    vals, idxs, sels = [], [], []
    cur = logits
    for _ in range(TOP_K):
        mx = jnp.max(cur, axis=0, keepdims=True)
        idx = jnp.min(jnp.where(cur == mx, e_iota, N_EXPERTS), axis=0, keepdims=True)
        sel = e_iota == idx
        vals.append(mx)
        idxs.append(idx)
        sels.append(sel)
        cur = jnp.where(sel, -jnp.inf, cur)
    ex = [jnp.exp(v - vals[0]) for v in vals]
    inv = 1.0 / (ex[0] + ex[1] + ex[2] + ex[3])
    onehot = jnp.where(sels[0] | sels[1] | sels[2] | sels[3], 1.0, 0.0)
    row = lax.broadcasted_iota(I32, (t, t), 0)
    col = lax.broadcasted_iota(I32, (t, t), 1)
    upper = jnp.where(row < col, 1.0, 0.0).astype(BF16)
    prefix = jnp.dot(onehot.astype(BF16), upper, preferred_element_type=F32) + carry_ref[:, 0:1]
    for k in range(TOP_K):
        idx_ref[k:k + 1, :] = idxs[k]
        gate_ref[k:k + 1, :] = ex[k] * inv
        rank_ref[k:k + 1, :] = jnp.sum(jnp.where(sels[k], prefix, 0.0), axis=0, keepdims=True).astype(I32)
    gate_ref[TOP_K:, :] = jnp.zeros((gate_ref.shape[0] - TOP_K, t), F32)
    r0_ref[0] = carry_ref[...]
    carry_ref[...] = carry_ref[...] + jnp.sum(onehot, axis=1, keepdims=True)
    cnt_ref[...] = carry_ref[...]


def _outproj(x2d, hg, attn, z, wa, wb, wo, g, wr2, br):
    n = x2d.shape[0]
    t = _tile(n, TOKEN_TILE)
    const = lambda i: (0, 0)
    once = pl.Buffered(1)
    return pl.pallas_call(
        _outproj_body,
        grid=(n // t,),
        in_specs=[
            pl.BlockSpec((t, D_MODEL), lambda i: (i, 0)),
            pl.BlockSpec((t, D_RNN), lambda i: (i, 0)),
            pl.BlockSpec((t, D_MODEL), lambda i: (i, 0)),
            pl.BlockSpec((t, D_MODEL), lambda i: (i, Z_GLA // D_MODEL)),
            pl.BlockSpec((t, D_MODEL), lambda i: (i, Z_GLB // D_MODEL)),
            pl.BlockSpec((D_RNN, D_MODEL), const, pipeline_mode=once),
            pl.BlockSpec((D_MODEL, D_MODEL), const, pipeline_mode=once),
            pl.BlockSpec((D_MODEL, D_MODEL), const, pipeline_mode=once),
            pl.BlockSpec((1, D_MODEL), const),
            pl.BlockSpec((2, D_MODEL, N_EXPERTS), lambda i: (0, 0, 0)),
            pl.BlockSpec((N_EXPERTS, 1), const),
        ],
        out_specs=[
            pl.BlockSpec((t, D_MODEL), lambda i: (i, 0)),
            pl.BlockSpec((TOP_K, t), lambda i: (0, i)),
            pl.BlockSpec((8, t), lambda i: (0, i)),
            pl.BlockSpec((TOP_K, t), lambda i: (0, i)),
            pl.BlockSpec((N_EXPERTS, LANES), const),
            pl.BlockSpec((1, N_EXPERTS, LANES), lambda i: (i, 0, 0)),
        ],
        out_shape=[
            jax.ShapeDtypeStruct((n, D_MODEL), F32),
            jax.ShapeDtypeStruct((TOP_K, n), I32),
            jax.ShapeDtypeStruct((8, n), F32),
            jax.ShapeDtypeStruct((TOP_K, n), I32),
            jax.ShapeDtypeStruct((N_EXPERTS, LANES), F32),
            jax.ShapeDtypeStruct((n // t, N_EXPERTS, LANES), F32),
        ],
        scratch_shapes=[pltpu.VMEM((N_EXPERTS, LANES), F32)],
        compiler_params=_cparams(("arbitrary",)),
        name="outproj",
    )(x2d, hg, attn, z, z, wa, wb, wo, g, wr2, br)


ZERO_ROWS = 256


def _dispatch_body(fs_ref, fe_ref, nu_ref, dest_ref, x1_ref, g_ref, xs_ref, un_ref, zero_ref, sem, zsem,
                   *, nsteps, nblocks):
    t = x1_ref.shape[0]
    i = pl.program_id(0)
    slot = i % 2

    def wait_slot(s):
        for _ in range(TOP_K):
            pltpu.make_async_copy(un_ref.at[s], xs_ref.at[pl.ds(0, t), :], sem.at[s]).wait()

    @pl.when(i >= 2)
    def _():
        wait_slot(slot)

    un_ref[slot] = _rms(x1_ref[...], g_ref[...])

    def issue(r, carry):
        for k in range(TOP_K):
            pltpu.make_async_copy(un_ref.at[slot, pl.ds(r, 1), :],
                                  xs_ref.at[pl.ds(dest_ref[k, r], 1), :], sem.at[slot]).start()
        return carry

    lax.fori_loop(0, t, issue, 0, unroll=8)

    @pl.when(i == 0)
    def _():
        zero_ref[...] = jnp.zeros(zero_ref.shape, F32)

        def fill_expert(e, carry):
            def start(r, c):
                pltpu.make_async_copy(zero_ref.at[pl.ds(0, 1), :], xs_ref.at[pl.ds(r, 1), :], zsem).start()
                return c

            def wait(r, c):
                pltpu.make_async_copy(zero_ref.at[pl.ds(0, 1), :], xs_ref.at[pl.ds(0, 1), :], zsem).wait()
                return c

            lax.fori_loop(fs_ref[e], fe_ref[e], start, 0)
            lax.fori_loop(fs_ref[e], fe_ref[e], wait, 0)
            return carry

        lax.fori_loop(0, N_EXPERTS, fill_expert, 0)

        def fill_block(b, carry):
            for part in range(MOE_ROWS // ZERO_ROWS):
                r0 = pl.multiple_of(b * MOE_ROWS + part * ZERO_ROWS, ZERO_ROWS)
                cp = pltpu.make_async_copy(zero_ref, xs_ref.at[pl.ds(r0, ZERO_ROWS), :], zsem)
                cp.start()
                cp.wait()
            return carry

        lax.fori_loop(nu_ref[0], nblocks, fill_block, 0)

    @pl.when(i == nsteps - 1)
    def _():
        wait_slot(slot)
        if nsteps >= 2:
            wait_slot(1 - slot)


def _dispatch(fill_start, fill_end, n_used, dest, x1, g, rows):
    n = x1.shape[0]
    t = _tile(n, 256)
    nsteps = n // t
    grid_spec = pltpu.PrefetchScalarGridSpec(
        num_scalar_prefetch=3,
        grid=(nsteps,),
        in_specs=[
            pl.BlockSpec((TOP_K, t), lambda i, fs, fe, nu: (0, i), memory_space=pltpu.SMEM),
            pl.BlockSpec((t, D_MODEL), lambda i, fs, fe, nu: (i, 0)),
            pl.BlockSpec((1, D_MODEL), lambda i, fs, fe, nu: (0, 0)),
        ],
        out_specs=pl.BlockSpec(memory_space=pl.ANY),
        scratch_shapes=[pltpu.VMEM((2, t, D_MODEL), F32), pltpu.VMEM((ZERO_ROWS, D_MODEL), F32),
                        pltpu.SemaphoreType.DMA((2,)), pltpu.SemaphoreType.DMA(())],
    )
    return pl.pallas_call(
        functools.partial(_dispatch_body, nsteps=nsteps, nblocks=rows // MOE_ROWS),
        grid_spec=grid_spec,
        out_shape=jax.ShapeDtypeStruct((rows, D_MODEL), F32),
        compiler_params=_cparams(("arbitrary",)),
        name="dispatch",
    )(fill_start, fill_end, n_used, dest, x1, g)


MOE_ROWS = 512
MOE_FF = 1024


def _experts_body(be_ref, nu_ref, xs_ref, wg_ref, wu_ref, bg_ref, bu_ref, wd_ref, bd_ref,
                  ys_ref, h_ref):
    i = pl.program_id(0)
    f = pl.program_id(1)
    nf = D_FF // MOE_FF

    @pl.when(jnp.logical_and(i >= nu_ref[0], f == 0))
    def _():
        ys_ref[...] = jnp.zeros(ys_ref.shape, BF16)

    @pl.when(i < nu_ref[0])
    def _():
        xb = xs_ref[...].astype(BF16)
        nt = (((1,), (1,)), ((), ()))
        g = lax.dot_general(xb, wg_ref[0], nt, preferred_element_type=F32) + bg_ref[0]
        u = lax.dot_general(xb, wu_ref[0], nt, preferred_element_type=F32) + bu_ref[0]
        g = jnp.minimum(g, SWIGLU_LIMIT)
        u = jnp.clip(u, -SWIGLU_LIMIT, SWIGLU_LIMIT)
        h = (g * _sigmoid(SWIGLU_ALPHA * g) * (u + 1.0)).astype(BF16)

        @pl.when(f < nf - 1)
        def _():
            h_ref[f] = h

        @pl.when(f == nf - 1)
        def _():
            hall = jnp.concatenate([h_ref[j] for j in range(nf - 1)] + [h], axis=1)
            ys_ref[...] = (jnp.dot(hall, wd_ref[0], preferred_element_type=F32) + bd_ref[0]).astype(BF16)


def _experts(block_e, n_used, xs, wg_t, wu_t, bg, bu, wd, bd):
    rows = xs.shape[0]
    tm = MOE_ROWS
    nb = rows // tm
    nf = D_FF // MOE_FF

    def blk(i, nu):
        return jnp.minimum(i, nu[0] - 1)

    def fidx(i, f, nu):
        return jnp.where(i < nu[0], f, nf - 1)

    grid_spec = pltpu.PrefetchScalarGridSpec(
        num_scalar_prefetch=2,
        grid=(nb, nf),
        in_specs=[
            pl.BlockSpec((tm, D_MODEL), lambda i, f, be, nu: (blk(i, nu), 0)),
            pl.BlockSpec((1, MOE_FF, D_MODEL), lambda i, f, be, nu: (be[blk(i, nu)], fidx(i, f, nu), 0)),
            pl.BlockSpec((1, MOE_FF, D_MODEL), lambda i, f, be, nu: (be[blk(i, nu)], fidx(i, f, nu), 0)),
            pl.BlockSpec((1, 1, MOE_FF), lambda i, f, be, nu: (be[blk(i, nu)], 0, fidx(i, f, nu))),
            pl.BlockSpec((1, 1, MOE_FF), lambda i, f, be, nu: (be[blk(i, nu)], 0, fidx(i, f, nu))),
            pl.BlockSpec((1, D_FF, D_MODEL), lambda i, f, be, nu: (be[blk(i, nu)], 0, 0)),
            pl.BlockSpec((1, 1, D_MODEL), lambda i, f, be, nu: (be[blk(i, nu)], 0, 0)),
        ],
        out_specs=pl.BlockSpec((tm, D_MODEL), lambda i, f, be, nu: (i, 0)),
        scratch_shapes=[pltpu.VMEM((nf - 1, tm, MOE_FF), BF16)],
    )
    return pl.pallas_call(
        _experts_body,
        grid_spec=grid_spec,
        out_shape=jax.ShapeDtypeStruct((rows, D_MODEL), BF16),
        compiler_params=_cparams(("arbitrary", "arbitrary")),
        name="experts",
    )(block_e, n_used, xs, wg_t, wu_t, bg, bu, wd, bd)


TOKEN_TILE = 256
WIN_ROWS = 64
WIN_ALIGN = 16


def _combine_body(base_ref, npass_ref, x1_ref, gate_ref, pos_ref, g_ref, ys_ref, o_ref,
                  win_ref, acc_ref, sem, *, nsteps, last_start):
    t = x1_ref.shape[0]
    kdim = N_EXPERTS * WIN_ROWS
    i = pl.program_id(0)
    slot = i % 2

    def window_copies(tile, p, dst_slot, dst_sem):
        for e in range(N_EXPERTS):
            start = jnp.minimum(base_ref[tile * N_EXPERTS + e] + p * WIN_ROWS, last_start)
            start = pl.multiple_of(start, WIN_ALIGN)
            yield pltpu.make_async_copy(ys_ref.at[pl.ds(start, WIN_ROWS), :],
                                        win_ref.at[dst_slot, pl.ds(e * WIN_ROWS, WIN_ROWS), :], dst_sem)

    def wait_windows(dst_slot, dst_sem):
        pltpu.make_async_copy(ys_ref.at[pl.ds(0, kdim), :], win_ref.at[dst_slot], dst_sem).wait()

    @pl.when(i == 0)
    def _():
        for cp in window_copies(0, 0, 0, sem.at[0]):
            cp.start()

    @pl.when(i + 1 < nsteps)
    def _():
        for cp in window_copies(i + 1, 0, 1 - slot, sem.at[1 - slot]):
            cp.start()

    gcol = gate_ref[...].T
    pcol = pos_ref[...].T
    lane = lax.broadcasted_iota(I32, (t, kdim), 1).astype(F32)

    def weights(p):
        w = jnp.zeros((t, kdim), F32)
        for k in range(TOP_K):
            pos_k = jnp.where(pcol[:, TOP_K + k:TOP_K + k + 1] == p, pcol[:, k:k + 1], -1.0)
            w = jnp.where(lane == pos_k, gcol[:, k:k + 1], w)
        return w.astype(BF16)

    wait_windows(slot, sem.at[slot])
    acc_ref[...] = x1_ref[...] + jnp.dot(weights(0.0), win_ref[slot], preferred_element_type=F32)

    def extra_pass(p, carry):
        for cp in window_copies(i, p, slot, sem.at[2]):
            cp.start()
        wait_windows(slot, sem.at[2])
        acc_ref[...] = acc_ref[...] + jnp.dot(weights(p.astype(F32)), win_ref[slot], preferred_element_type=F32)
        return carry

    lax.fori_loop(1, npass_ref[i], extra_pass, 0)
    o_ref[...] = _rms(acc_ref[...], g_ref[...])


def _combine(base_al, npass, x1, gate, posf, g, ys):
    n = x1.shape[0]
    rows = ys.shape[0]
    t = _tile(n, TOKEN_TILE)
    nsteps = n // t
    kdim = N_EXPERTS * WIN_ROWS
    grid_spec = pltpu.PrefetchScalarGridSpec(
        num_scalar_prefetch=2,
        grid=(nsteps,),
        in_specs=[
            pl.BlockSpec((t, D_MODEL), lambda i, b, q: (i, 0)),
            pl.BlockSpec((8, t), lambda i, b, q: (0, i)),
            pl.BlockSpec((8, t), lambda i, b, q: (0, i)),
            pl.BlockSpec((1, D_MODEL), lambda i, b, q: (0, 0)),
            pl.BlockSpec(memory_space=pl.ANY),
        ],
        out_specs=pl.BlockSpec((t, D_MODEL), lambda i, b, q: (i, 0)),
        scratch_shapes=[pltpu.VMEM((2, kdim, D_MODEL), BF16), pltpu.VMEM((t, D_MODEL), F32),
                        pltpu.SemaphoreType.DMA((3,))],
    )
    return pl.pallas_call(
        functools.partial(_combine_body, nsteps=nsteps, last_start=rows - WIN_ROWS),
        grid_spec=grid_spec,
        out_shape=jax.ShapeDtypeStruct((n, D_MODEL), F32),
        compiler_params=_cparams(("arbitrary",)),
        name="combine",
    )(base_al, npass, x1, gate, posf, g, ys)


PREP_K = 512


def _prep_gate_up_body(w_ref, g_ref, u_ref, t_ref):
    wt = w_ref[0].T
    for c in range(PREP_K // LANES):
        cols = slice(c * LANES, (c + 1) * LANES)
        t_ref[c] = wt[:, cols]
        g_ref[0, :, cols] = t_ref[c, pl.ds(0, D_FF, stride=2), :].astype(BF16)
        u_ref[0, :, cols] = t_ref[c, pl.ds(1, D_FF, stride=2), :].astype(BF16)


def _prep_gate_up(w_gate_up):
    out = jax.ShapeDtypeStruct((N_EXPERTS, D_FF, D_MODEL), BF16)
    return pl.pallas_call(
        _prep_gate_up_body,
        grid=(N_EXPERTS, D_MODEL // PREP_K),
        in_specs=[pl.BlockSpec((1, PREP_K, 2 * D_FF), lambda e, k: (e, k, 0))],
        out_specs=[pl.BlockSpec((1, D_FF, PREP_K), lambda e, k: (e, 0, k)),
                   pl.BlockSpec((1, D_FF, PREP_K), lambda e, k: (e, 0, k))],
        out_shape=[out, out],
        scratch_shapes=[pltpu.VMEM((PREP_K // LANES, 2 * D_FF, LANES), F32)],
        compiler_params=_cparams(("arbitrary", "arbitrary")),
        name="prep_gate_up",
    )(w_gate_up)


def _cast_body(w_ref, o_ref):
    o_ref[...] = w_ref[...].astype(BF16)


def _prep_cast(w):
    e, r, c = w.shape
    tr = _tile(r, 1024)
    return pl.pallas_call(
        _cast_body,
        grid=(e, r // tr),
        in_specs=[pl.BlockSpec((1, tr, c), lambda i, j: (i, j, 0))],
        out_specs=pl.BlockSpec((1, tr, c), lambda i, j: (i, j, 0)),
        out_shape=jax.ShapeDtypeStruct(w.shape, BF16),
        compiler_params=_cparams(("arbitrary", "arbitrary")),
        name="prep_cast",
    )(w)


def _rot_cols(w):
    half = QK_ROPE // 2
    return jnp.concatenate([-w[..., half:], w[..., :half]], axis=-1)


def _prep_weights(norm_mix_g, w_in, conv_w, conv_b, rg_w_a, rg_b_a, rg_w_x, rg_b_x, rg_lambda,
                  w_proj_a, q_norm_g, w_uq, kv_norm_g, w_uk, w_uv, w_proj_b, w_out,
                  norm_ffn_g, w_router, b_router, w_gate_up, b_gate_up, w_down, b_down, norm_final_g):
    p = {}
    wi = w_in[0]
    o_xa, o_ga, o_cq, o_ckv, o_kr, o_gla, o_glb = 0, 1024, 2048, 2560, 3072, 3136, 5184
    p["w_main"] = jnp.concatenate([
        wi[:, o_gla:o_gla + D_MODEL], wi[:, o_glb:o_glb + D_MODEL],
        wi[:, o_xa:o_xa + D_RNN], wi[:, o_ga:o_ga + D_RNN],
        wi[:, o_cq:o_cq + Q_LORA], wi[:, o_ckv:o_ckv + KV_LORA]], axis=1).astype(BF16)
    wkr = wi[:, o_kr:o_kr + QK_ROPE]
    p["w_k2"] = jnp.concatenate([wkr, _rot_cols(wkr)], axis=1).astype(BF16)
    p["g_mix"] = norm_mix_g[0].reshape(1, D_MODEL)
    p["conv_w"] = conv_w[0]
    p["conv_b"] = conv_b[0].reshape(1, D_RNN)
    wa, wx = rg_w_a[0], rg_w_x[0]
    p["w4"] = jnp.concatenate([wa[0], wx[0], wa[1], wx[1]], axis=-1).astype(BF16)
    ba = rg_b_a[0].reshape(2, RNN_BLOCKS, 1, BLOCK_W)
    bx = rg_b_x[0].reshape(2, RNN_BLOCKS, 1, BLOCK_W)
    p["b4"] = jnp.concatenate([ba[0], bx[0], ba[1], bx[1]], axis=-1)
    lam = rg_lambda[0].reshape(2, RNN_BLOCKS, 1, BLOCK_W)
    p["lam4"] = jnp.concatenate([lam[0], lam[1]], axis=-1)
    p["w_a"] = w_proj_a[0].astype(BF16)
    p["q_g"] = q_norm_g[0].reshape(1, Q_LORA)
    p["kv_g"] = kv_norm_g[0].reshape(1, KV_LORA)
    wq = w_uq[0].reshape(Q_LORA, N_HEADS, QK_HEAD)
    p["wq_n"] = wq[:, :, :QK_NOPE].reshape(Q_LORA, N_HEADS * QK_NOPE).astype(BF16)
    wqr = wq[:, :, QK_NOPE:]
    p["wq_r"] = jnp.concatenate([wqr, _rot_cols(wqr)], axis=-1).reshape(Q_LORA, N_HEADS * LANES).astype(BF16)
    p["w_k"] = w_uk[0].astype(BF16)
    p["w_vt"] = w_uv[0].T.astype(BF16)
    p["w_b"] = w_proj_b[0].astype(BF16)
    p["w_o"] = w_out[0].astype(BF16)
    p["g_ffn"] = norm_ffn_g[0].reshape(1, D_MODEL)
    w_r_hi = w_router[0].astype(BF16)
    w_r_lo = (w_router[0] - w_r_hi.astype(F32)).astype(BF16)
    p["w_r2"] = jnp.stack([w_r_hi, w_r_lo])
    p["b_r"] = b_router[0].reshape(N_EXPERTS, 1)
    p["w_g_t"], p["w_u_t"] = _prep_gate_up(w_gate_up[0])
    bgu = b_gate_up[0]
    p["b_g"] = bgu[:, 0::2].reshape(N_EXPERTS, 1, D_FF)
    p["b_u"] = bgu[:, 1::2].reshape(N_EXPERTS, 1, D_FF)
    p["w_d"] = _prep_cast(w_down[0])
    p["b_d"] = b_down[0].reshape(N_EXPERTS, 1, D_MODEL)
    p["g_fin"] = norm_final_g.reshape(1, D_MODEL)
    return p


def _rope_tables(seq):
    half = QK_ROPE // 2
    freqs = ROPE_THETA ** (-jnp.arange(half, dtype=F32) / half)
    ang = jnp.arange(seq, dtype=F32)[:, None] * freqs[None, :]
    zero = jnp.zeros((seq, LANES - QK_ROPE), F32)
    ctab = jnp.concatenate([jnp.cos(ang), jnp.cos(ang), zero], axis=1)
    stab = jnp.concatenate([jnp.sin(ang), jnp.sin(ang), zero], axis=1)
    return ctab, stab


def _trunk(x, p):
    batch, seq, _ = x.shape
    n = batch * seq
    x2d = x.reshape(n, D_MODEL)
    z, zk = _in_proj(x2d, p["g_mix"], p["w_main"], p["w_k2"])
    hg = _rglru(z, p["conv_w"], p["conv_b"], p["w4"], p["b4"], p["lam4"], batch, seq)
    ctab, stab = _rope_tables(seq)
    qn, qr, kn, kr, vt = _qkv(z, zk, ctab, stab, p["q_g"], p["kv_g"],
                              p["wq_n"], p["wq_r"], p["w_k"], p["w_vt"], seq)
    attn = _attention(qn, qr, kn, kr, vt, batch, seq)
    x1, idx, gate, rank, cnt, r0 = _outproj(x2d, hg, attn, z, p["w_a"], p["w_b"], p["w_o"],
                                        p["g_ffn"], p["w_r2"], p["b_r"])

    tm = MOE_ROWS
    counts = cnt[:, 0].astype(I32)
    padded = (counts + tm - 1) // tm * tm
    pad_end = jnp.cumsum(padded)
    pad_start = pad_end - padded
    experts = jnp.arange(N_EXPERTS, dtype=I32)
    dest = jnp.sum(jnp.where(idx[:, :, None] == experts, pad_start, 0), axis=-1) + rank
    nb = -(-(n * TOP_K) // tm) + N_EXPERTS
    rows = nb * tm
    blk_start = jnp.arange(nb, dtype=I32) * tm
    block_e = jnp.minimum(jnp.sum((pad_end[None, :] <= blk_start[:, None]).astype(I32), axis=1), N_EXPERTS - 1)
    n_used = (pad_end[-1:] // tm).astype(I32)

    xs = _dispatch(pad_start + counts, pad_end, n_used, dest, x1, p["g_ffn"], rows)
    ys = _experts(block_e, n_used, xs, p["w_g_t"], p["w_u_t"], p["b_g"], p["b_u"], p["w_d"], p["b_d"])

    t = _tile(n, TOKEN_TILE)
    last_start = rows - WIN_ROWS
    run_start = pad_start[None, :] + r0[:, :, 0].astype(I32)
    base_al = jnp.minimum(run_start // WIN_ALIGN * WIN_ALIGN, last_start)
    tok_base = jnp.repeat(base_al, t, axis=0)
    base_k = jnp.sum(jnp.where(idx[:, :, None] == experts, tok_base[None], 0), axis=-1)
    pass_k = (dest - base_k) // WIN_ROWS
    start_k = jnp.minimum(base_k + pass_k * WIN_ROWS, last_start)
    pos_k = idx * WIN_ROWS + (dest - start_k)
    posf = jnp.concatenate([pos_k, pass_k], axis=0).astype(F32)
    npass = (jnp.max(pass_k.reshape(TOP_K, n // t, t), axis=(0, 2)) + 1).astype(I32)
    out = _combine(base_al.reshape(-1), npass, x1, gate, posf, p["g_fin"], ys)
    return out.reshape(batch, seq, D_MODEL)


def kernel(x_prompt, x_sample, norm_mix_g, w_in, conv_w, conv_b, rg_w_a, rg_b_a, rg_w_x, rg_b_x, rg_lambda,
           w_proj_a, q_norm_g, w_uq, kv_norm_g, w_uk, w_uv, w_proj_b, w_out, norm_ffn_g, w_router, b_router,
           w_gate_up, b_gate_up, w_down, b_down, norm_final_g):
    p = _prep_weights(norm_mix_g, w_in, conv_w, conv_b, rg_w_a, rg_b_a, rg_w_x, rg_b_x, rg_lambda,
                      w_proj_a, q_norm_g, w_uq, kv_norm_g, w_uk, w_uv, w_proj_b, w_out,
                      norm_ffn_g, w_router, b_router, w_gate_up, b_gate_up, w_down, b_down, norm_final_g)
    return (_trunk(x_prompt, p), _trunk(x_sample, p))
```
